```python
import jax, jax.numpy as jnp
from jax import lax
import numpy as np

D_MODEL = 1024
BATCH = 4
SEQ = 8192
DEPTH = 1
DEC_BATCH = 128
DEC_SEQ = 8
PAST_LEN = 8192
PAGE_SIZE = 128

N_HEADS = 8
HEAD_DIM = 64
ATTN_DIM = N_HEADS * HEAD_DIM
CONV_DIM = D_MODEL - ATTN_DIM
CONV_GROUPS = 8
MIX_DIM = ATTN_DIM + CONV_DIM
IN_DIM = 3 * ATTN_DIM + 2 * CONV_DIM
CONV_W = 31
MOBA_BLOCK = 256
MOBA_TOPK = 3
Q_BLOCK = 64
D_FF = 2816
FFN_CONV_W = 3
EPS = 1e-6
NEG = -1e30

kernel_name = "hymba_moba_conformer_convffn_step"


def rmsnorm(x, g):
    xf = x.astype(jnp.float32)
    y = xf * lax.rsqrt(jnp.mean(xf * xf, axis=-1, keepdims=True) + EPS)
    return (y * g.astype(jnp.float32)).astype(x.dtype)


def group_rmsnorm(x, g, n_groups):
    shp = x.shape
    xf = x.astype(jnp.float32).reshape(*shp[:-1], n_groups, shp[-1] // n_groups)
    y = xf * lax.rsqrt(jnp.mean(xf * xf, axis=-1, keepdims=True) + EPS)
    return (y.reshape(shp) * g.astype(jnp.float32)).astype(x.dtype)


def layernorm(x, g, b):
    xf = x.astype(jnp.float32)
    mu = jnp.mean(xf, axis=-1, keepdims=True)
    var = jnp.mean(jnp.square(xf - mu), axis=-1, keepdims=True)
    y = (xf - mu) * lax.rsqrt(var + EPS) * g.astype(jnp.float32) + b.astype(jnp.float32)
    return y.astype(x.dtype)


def alibi_slopes():
    return jnp.exp2(-8.0 * jnp.arange(1, N_HEADS + 1, dtype=jnp.float32) / N_HEADS)


def causal_dwconv(u_ctx, w, b):
    c = u_ctx.shape[-1]
    out = lax.conv_general_dilated(u_ctx, w[:, None, :], (1,), 'VALID',
                                   dimension_numbers=('NWC', 'WIO', 'NWC'),
                                   feature_group_count=c)
    return out + b


def mix_in(x, g_pre, w_in):
    h = rmsnorm(x, g_pre)
    z = h @ w_in
    q, k, v, a, g = jnp.split(z, [ATTN_DIM, 2 * ATTN_DIM, 3 * ATTN_DIM, 3 * ATTN_DIM + CONV_DIM], axis=-1)
    hs = lambda t: t.reshape(*t.shape[:-1], N_HEADS, HEAD_DIM)
    return hs(q), hs(k), hs(v), a, g


def moba_select(q, kmean, n_past):
    scores = jnp.einsum('bqhd,bnhd->bhqn', q.astype(jnp.float32), kmean)
    cand = jnp.arange(kmean.shape[1]) < n_past
    scores = jnp.where(cand, scores, NEG)
    _, idx = lax.top_k(scores, MOBA_TOPK)
    valid = jnp.arange(MOBA_TOPK) < n_past
    return idx, valid


def moba_attend(q, t, sel_k, sel_v, sel_start, sel_valid, own_k, own_v, own_pos, slopes):
    qf = q * (HEAD_DIM ** -0.5)
    l_sel = jnp.einsum('bqhd,bhqkpd->bhqkp', qf, sel_k, preferred_element_type=jnp.float32)
    sel_pos = sel_start[..., None] + jnp.arange(MOBA_BLOCK)
    d_sel = (t[None, None, :, None, None] - sel_pos).astype(jnp.float32)
    l_sel = l_sel - slopes[None, :, None, None, None] * d_sel
    l_sel = jnp.where(sel_valid[None, None, None, :, None], l_sel, NEG)
    l_own = jnp.einsum('bqhd,bshd->bhqs', qf, own_k, preferred_element_type=jnp.float32)
    d_own = (t[:, None] - own_pos[None, :]).astype(jnp.float32)
    l_own = l_own - slopes[:, None, None] * d_own
    l_own = jnp.where(own_pos[None, :] <= t[:, None], l_own, NEG)
    b, h, nq, nk, bs = l_sel.shape
    p = jax.nn.softmax(jnp.concatenate([l_sel.reshape(b, h, nq, nk * bs), l_own], axis=-1), axis=-1)
    p_sel = p[..., :nk * bs].reshape(b, h, nq, nk, bs).astype(sel_v.dtype)
    p_own = p[..., nk * bs:].astype(own_v.dtype)
    return (jnp.einsum('bhqkp,bhqkpd->bqhd', p_sel, sel_v)
            + jnp.einsum('bhqs,bshd->bqhd', p_own, own_v))


def moba_prompt(q, k, v, slopes):
    B, S, H, Dh = q.shape
    nb = max(-(-S // MOBA_BLOCK), MOBA_TOPK)
    pad = nb * MOBA_BLOCK - S
    k_pad = jnp.pad(k, ((0, 0), (0, pad), (0, 0), (0, 0)))
    v_pad = jnp.pad(v, ((0, 0), (0, pad), (0, 0), (0, 0)))
    kb = k_pad.reshape(B, nb, MOBA_BLOCK, H, Dh)
    kmean = jnp.mean(kb, axis=2, dtype=jnp.float32)
    kbh = kb.transpose(0, 3, 1, 2, 4)
    vbh = v_pad.reshape(B, nb, MOBA_BLOCK, H, Dh).transpose(0, 3, 1, 2, 4)
    nq = S // Q_BLOCK
    qc = q.reshape(B, nq, Q_BLOCK, H, Dh).transpose(1, 0, 2, 3, 4)
    bidx = jnp.arange(B)[:, None, None, None]
    hidx = jnp.arange(H)[None, :, None, None]

    def one_block(args):
        c, q_c = args
        start = c * Q_BLOCK
        j = start // MOBA_BLOCK
        t = start + jnp.arange(Q_BLOCK)
        idx, valid = moba_select(q_c, kmean, j)
        sel_k = kbh[bidx, hidx, idx]
        sel_v = vbh[bidx, hidx, idx]
        own_start = j * MOBA_BLOCK
        own_k = lax.dynamic_slice_in_dim(k_pad, own_start, MOBA_BLOCK, axis=1)
        own_v = lax.dynamic_slice_in_dim(v_pad, own_start, MOBA_BLOCK, axis=1)
        own_pos = own_start + jnp.arange(MOBA_BLOCK)
        return moba_attend(q_c, t, sel_k, sel_v, idx * MOBA_BLOCK, valid, own_k, own_v, own_pos, slopes)

    out = lax.map(one_block, (jnp.arange(nq), qc))
    return out.transpose(1, 0, 2, 3, 4).reshape(B, S, H, Dh)


def moba_sample(q, k, v, k_pool, v_pool, page_table, slopes):
    DB, T, H, Dh = q.shape
    ppb = MOBA_BLOCK // PAGE_SIZE
    n_pages = page_table.shape[1]
    past = n_pages * PAGE_SIZE
    n_full = past // MOBA_BLOCK
    nbc = max(n_full, MOBA_TOPK)
    lp = jnp.minimum(jnp.arange(nbc * ppb), n_pages - 1)
    kg = k_pool[page_table[:, lp]]
    kmean = jnp.mean(kg.reshape(DB, nbc, MOBA_BLOCK, H, Dh), axis=2, dtype=jnp.float32)
    idx, valid = moba_select(q, kmean, n_full)
    tail_len = past - n_full * MOBA_BLOCK
    tail_lp = jnp.arange(n_full * ppb, n_pages)
    tail_k = k_pool[page_table[:, tail_lp]].reshape(DB, tail_len, H, Dh)
    tail_v = v_pool[page_table[:, tail_lp]].reshape(DB, tail_len, H, Dh)
    own_k = jnp.concatenate([tail_k, k], axis=1)
    own_v = jnp.concatenate([tail_v, v], axis=1)
    own_pos = n_full * MOBA_BLOCK + jnp.arange(tail_len + T)
    t_all = past + jnp.arange(T)
    bidx4 = jnp.arange(DB)[:, None, None, None]
    hidx5 = jnp.arange(H)[None, :, None, None, None]
    rows = jnp.arange(PAGE_SIZE)
    page_off = jnp.arange(ppb)

    def one_token(args):
        q_i, idx_i, t_i = args
        lp_i = jnp.minimum(idx_i[..., None] * ppb + page_off, n_pages - 1)
        phys = page_table[bidx4, lp_i]
        sel_k = k_pool[phys[..., None], rows, hidx5].reshape(DB, H, 1, MOBA_TOPK, MOBA_BLOCK, Dh)
        sel_v = v_pool[phys[..., None], rows, hidx5].reshape(DB, H, 1, MOBA_TOPK, MOBA_BLOCK, Dh)
        o = moba_attend(q_i[:, None], t_i[None], sel_k, sel_v, idx_i[:, :, None, :] * MOBA_BLOCK,
                        valid, own_k, own_v, own_pos, slopes)
        return o[:, 0]

    out = lax.map(one_token, (q.transpose(1, 0, 2, 3), idx.transpose(2, 0, 1, 3), t_all))
    return out.transpose(1, 0, 2, 3)


def conformer_conv(a, g, ctx, w_dw, b_dw, ln_g, ln_b):
    u = a * jax.nn.sigmoid(g)
    u_ctx = jnp.concatenate([ctx, u], axis=1)
    c = layernorm(causal_dwconv(u_ctx, w_dw, b_dw), ln_g, ln_b)
    return jax.nn.silu(c), u_ctx[:, -(CONV_W - 1):]


def merge_out(x, attn, conv, g_attn_out, g_conv_out, w_out, g_post):
    attn = attn.reshape(*attn.shape[:-2], ATTN_DIM)
    o = jnp.concatenate([group_rmsnorm(attn, g_attn_out, N_HEADS),
                         group_rmsnorm(conv, g_conv_out, CONV_GROUPS)], axis=-1) @ w_out
    return x + rmsnorm(o, g_post)


def conv_ffn(x, ctx, g_pre, w_up, w_dw, b_dw, w_down, g_post):
    up = rmsnorm(x, g_pre) @ w_up
    up_ctx = jnp.concatenate([ctx, up], axis=1)
    gate, val = jnp.split(causal_dwconv(up_ctx, w_dw, b_dw), 2, axis=-1)
    f = (jax.nn.gelu(gate, approximate=True) * val) @ w_down
    return x + rmsnorm(f, g_post), up_ctx[:, -(FFN_CONV_W - 1):]


def setup_inputs(seed: int = 0) -> dict:
    key = jax.random.key(seed)
    ks = jax.random.split(key, 32)
    n_pages = PAST_LEN // PAGE_SIZE
    n_used = DEC_BATCH * n_pages
    n_pool = n_used + max(1, n_used // 4)
    f32 = jnp.float32
    nrm = lambda k, shp, s: jax.random.normal(k, shp, f32) * s
    gain = lambda k, n: 1.0 + 0.02 * jax.random.normal(k, (DEPTH, n), f32)
    page_table = jax.random.permutation(ks[4], n_pool)[:n_used].reshape(DEC_BATCH, n_pages).astype(jnp.int32)
    return {
        "x_prompt": nrm(ks[0], (BATCH, SEQ, D_MODEL), 1.0),
        "x_sample": nrm(ks[1], (DEC_BATCH, DEC_SEQ, D_MODEL), 1.0),
        "cache_k": nrm(ks[2], (DEPTH, n_pool, PAGE_SIZE, N_HEADS, HEAD_DIM), 1.0),
        "cache_v": nrm(ks[3], (DEPTH, n_pool, PAGE_SIZE, N_HEADS, HEAD_DIM), 1.0),
        "page_table": page_table,
        "state_conv": nrm(ks[5], (DEPTH, DEC_BATCH, CONV_W - 1, CONV_DIM), 0.5),
        "state_ffn_conv": nrm(ks[6], (DEPTH, DEC_BATCH, FFN_CONV_W - 1, 2 * D_FF), 1.0),
        "g_pre_mix": gain(ks[7], D_MODEL),
        "w_in": nrm(ks[8], (DEPTH, D_MODEL, IN_DIM), D_MODEL ** -0.5),
        "w_conv_dw": nrm(ks[9], (DEPTH, CONV_W, CONV_DIM), CONV_W ** -0.5),
        "b_conv_dw": nrm(ks[10], (DEPTH, CONV_DIM), 0.02),
        "g_conv_ln": gain(ks[11], CONV_DIM),
        "b_conv_ln": nrm(ks[12], (DEPTH, CONV_DIM), 0.02),
        "g_attn_out": gain(ks[13], ATTN_DIM),
        "g_conv_out": gain(ks[14], CONV_DIM),
        "w_out": nrm(ks[15], (DEPTH, MIX_DIM, D_MODEL), MIX_DIM ** -0.5),
        "g_post_mix": gain(ks[16], D_MODEL),
        "g_pre_ffn": gain(ks[17], D_MODEL),
        "w_up": nrm(ks[18], (DEPTH, D_MODEL, 2 * D_FF), D_MODEL ** -0.5),
        "w_ffn_dw": nrm(ks[19], (DEPTH, FFN_CONV_W, 2 * D_FF), FFN_CONV_W ** -0.5),
        "b_ffn_dw": nrm(ks[20], (DEPTH, 2 * D_FF), 0.02),
        "w_down": nrm(ks[21], (DEPTH, D_FF, D_MODEL), D_FF ** -0.5),
        "g_post_ffn": gain(ks[22], D_MODEL),
    }


def reference(x_prompt, x_sample, cache_k, cache_v, page_table, state_conv, state_ffn_conv,
              g_pre_mix, w_in, w_conv_dw, b_conv_dw, g_conv_ln, b_conv_ln, g_attn_out, g_conv_out,
              w_out, g_post_mix, g_pre_ffn, w_up, w_ffn_dw, b_ffn_dw, w_down, g_post_ffn):
    slopes = alibi_slopes()
    xp, xs = x_prompt, x_sample
    kp_l, vp_l, cp_l, fp_l, ks_l, vs_l, cs_l, fs_l = [], [], [], [], [], [], [], []
    for l in range(DEPTH):
        qp, kp, vp, ap, gp = mix_in(xp, g_pre_mix[l], w_in[l])
        attn_p = moba_prompt(qp, kp, vp, slopes)
        conv_p, cst_p = conformer_conv(ap, gp, jnp.zeros((xp.shape[0], CONV_W - 1, CONV_DIM), ap.dtype),
                                       w_conv_dw[l], b_conv_dw[l], g_conv_ln[l], b_conv_ln[l])
        xp = merge_out(xp, attn_p, conv_p, g_attn_out[l], g_conv_out[l], w_out[l], g_post_mix[l])
        xp, fst_p = conv_ffn(xp, jnp.zeros((xp.shape[0], FFN_CONV_W - 1, 2 * D_FF), xp.dtype),
                             g_pre_ffn[l], w_up[l], w_ffn_dw[l], b_ffn_dw[l], w_down[l], g_post_ffn[l])
        qs, ks_, vs_, as_, gs_ = mix_in(xs, g_pre_mix[l], w_in[l])
        attn_s = moba_sample(qs, ks_, vs_, cache_k[l], cache_v[l], page_table, slopes)
        conv_s, cst_s = conformer_conv(as_, gs_, state_conv[l], w_conv_dw[l], b_conv_dw[l],
                                       g_conv_ln[l], b_conv_ln[l])
        xs = merge_out(xs, attn_s, conv_s, g_attn_out[l], g_conv_out[l], w_out[l], g_post_mix[l])
        xs, fst_s = conv_ffn(xs, state_ffn_conv[l], g_pre_ffn[l], w_up[l], w_ffn_dw[l], b_ffn_dw[l],
                             w_down[l], g_post_ffn[l])
        kp_l.append(kp); vp_l.append(vp); cp_l.append(cst_p); fp_l.append(fst_p)
        ks_l.append(ks_); vs_l.append(vs_); cs_l.append(cst_s); fs_l.append(fst_s)
    return (xp, xs, jnp.stack(kp_l), jnp.stack(vp_l), jnp.stack(cp_l), jnp.stack(fp_l),
            jnp.stack(ks_l), jnp.stack(vs_l), jnp.stack(cs_l), jnp.stack(fs_l))
```

```python
import functools

import jax
import jax.numpy as jnp
from jax import lax
from jax.experimental import pallas as pl
from jax.experimental.pallas import tpu as pltpu

N_HEADS = 8
HEAD_DIM = 64
ATTN_DIM = N_HEADS * HEAD_DIM
CONV_GROUPS = 8
CONV_W = 31
MOBA_BLOCK = 256
MOBA_TOPK = 3
FFN_CONV_W = 3
EPS = 1e-6
NEG = -1e30

LANES = 128
BF16_SUBLANES = 16
HALO = 32
PAGES_PER_STEP = 16
VMEM_LIMIT = 56 * 1024 * 1024

F32 = jnp.float32
BF16 = jnp.bfloat16
HIGHEST = lax.Precision.HIGHEST


def _round_up(n, m):
    return -(-n // m) * m


def _const_spec(shape):
    zeros = (0,) * len(shape)
    return pl.BlockSpec(shape, lambda *_: zeros)


def _rms(x, g):
    return x * lax.rsqrt(jnp.mean(x * x, axis=-1, keepdims=True) + EPS) * g


def _split_dot(x, w_bf16):
    hi = x.astype(BF16)
    lo = (x - hi.astype(F32)).astype(BF16)
    return (jnp.dot(hi, w_bf16, preferred_element_type=F32)
            + jnp.dot(lo, w_bf16, preferred_element_type=F32))


def _top_k_bias(sc, cand, axis):
    n = sc.shape[axis]
    idx = lax.broadcasted_iota(jnp.int32, sc.shape, axis)
    s = jnp.where(cand, sc, -jnp.inf)
    bias = jnp.full(sc.shape, NEG, F32)
    for _ in range(MOBA_TOPK):
        m = jnp.max(s, axis=axis, keepdims=True)
        first = jnp.min(jnp.where(s == m, idx, n), axis=axis, keepdims=True)
        pick = idx == first
        bias = jnp.where(pick, 0.0, bias)
        s = jnp.where(pick, -jnp.inf, s)
    return jnp.where(cand, bias, NEG)


def _mix_prompt_kernel(x_ref, g_ref, w_ref, qT_ref, selT_ref, kbf_ref, vT_ref, k_ref, v_ref, u_ref,
                       km_ref):
    j = pl.program_id(1)
    nbp = km_ref.shape[0]

    @pl.when(j == 0)
    def _():
        km_ref[...] = jnp.zeros_like(km_ref)

    h = _rms(x_ref[0], g_ref[...]).astype(BF16)
    z = jnp.dot(h, w_ref[...], preferred_element_type=F32)
    q = z[:, 0:ATTN_DIM] * (HEAD_DIM ** -0.5)
    k = z[:, ATTN_DIM:2 * ATTN_DIM]
    v = z[:, 2 * ATTN_DIM:3 * ATTN_DIM]
    a = z[:, 3 * ATTN_DIM:3 * ATTN_DIM + 512]
    gg = z[:, 3 * ATTN_DIM + 512:]
    k_ref[0] = k
    v_ref[0] = v
    kbf_ref[0] = k.astype(BF16)
    u_ref[0] = a * jax.nn.sigmoid(gg)
    qT = q.T
    qT_ref[0] = qT.astype(BF16)
    vT_ref[0, 0] = v.T.astype(BF16)

    km = km_ref[...]
    blk = lax.broadcasted_iota(jnp.int32, (nbp, MOBA_BLOCK), 0)
    lane_head = lax.broadcasted_iota(jnp.int32, (nbp, LANES), 1) // HEAD_DIM
    for hh in range(N_HEADS):
        pr = hh // 2
        kmp = jnp.where(lane_head == hh % 2, km[:, pr * LANES:(pr + 1) * LANES], 0.0)
        sc = jnp.dot(kmp, qT[pr * LANES:(pr + 1) * LANES, :], preferred_element_type=F32,
                     precision=HIGHEST)
        selT_ref[0, hh] = jnp.where(blk == j, 0.0, _top_k_bias(sc, blk < j, 0)).astype(BF16)

    row = lax.broadcasted_iota(jnp.int32, km.shape, 0)
    km_ref[...] = jnp.where(row == j, jnp.mean(k, axis=0, keepdims=True), km)


def _mix_prompt(x, g, w_bf16, nbp):
    B, S, D = x.shape
    nb = S // MOBA_BLOCK
    n_out = w_bf16.shape[1]
    tile = lambda b, j: (b, j, 0)
    return pl.pallas_call(
        _mix_prompt_kernel,
        grid=(B, nb),
        in_specs=[pl.BlockSpec((1, MOBA_BLOCK, D), tile), _const_spec((1, D)), _const_spec((D, n_out))],
        out_specs=[
            pl.BlockSpec((1, ATTN_DIM, MOBA_BLOCK), lambda b, j: (b, 0, j)),
            pl.BlockSpec((1, N_HEADS, nbp, MOBA_BLOCK), lambda b, j: (b, 0, 0, j)),
            pl.BlockSpec((1, MOBA_BLOCK, ATTN_DIM), tile),
            pl.BlockSpec((1, 1, ATTN_DIM, MOBA_BLOCK), lambda b, j: (b, j, 0, 0)),
            pl.BlockSpec((1, MOBA_BLOCK, ATTN_DIM), tile),
            pl.BlockSpec((1, MOBA_BLOCK, ATTN_DIM), tile),
            pl.BlockSpec((1, MOBA_BLOCK, 512), tile),
        ],
        out_shape=[
            jax.ShapeDtypeStruct((B, ATTN_DIM, S), BF16),
            jax.ShapeDtypeStruct((B, N_HEADS, nbp, S), BF16),
            jax.ShapeDtypeStruct((B, S, ATTN_DIM), BF16),
            jax.ShapeDtypeStruct((B, nb, ATTN_DIM, MOBA_BLOCK), BF16),
            jax.ShapeDtypeStruct((B, S, ATTN_DIM), F32),
            jax.ShapeDtypeStruct((B, S, ATTN_DIM), F32),
            jax.ShapeDtypeStruct((B, S, 512), F32),
        ],
        scratch_shapes=[pltpu.VMEM((nbp, ATTN_DIM), F32)],
        compiler_params=pltpu.CompilerParams(dimension_semantics=("arbitrary", "arbitrary"),
                                             vmem_limit_bytes=VMEM_LIMIT),
        name="mix_prompt",
    )(x, g, w_bf16)


def _attn_prompt_kernel(qT_ref, selT_ref, cq_ref, k_ref, vT_ref, kb_ref, g_ref, o_ref):
    j = pl.program_id(2)
    qT = qT_ref[0]
    sub_head = lax.broadcasted_iota(jnp.int32, qT.shape, 0) // HEAD_DIM
    qq = [jnp.concatenate([jnp.where(sub_head == hh, qT, jnp.zeros_like(qT)), selT_ref[0, hh], cq_ref[hh]],
                          axis=0) for hh in range(2)]
    key_i = lax.broadcasted_iota(jnp.int32, (MOBA_BLOCK, MOBA_BLOCK), 0)
    qry_i = lax.broadcasted_iota(jnp.int32, (MOBA_BLOCK, MOBA_BLOCK), 1)

    def step(n, carry, diagonal):
        off = pl.multiple_of(n * MOBA_BLOCK, MOBA_BLOCK)
        kk = jnp.concatenate([k_ref[0, pl.ds(off, MOBA_BLOCK), :], kb_ref[pl.ds(off, MOBA_BLOCK), :]], axis=1)
        vt = vT_ref[0, n]
        out = []
        for hh in range(2):
            m, l, acc = carry[hh]
            s = jnp.dot(kk, qq[hh], preferred_element_type=F32)
            if diagonal:
                s = jnp.where(key_i <= qry_i, s, NEG)
            m_new = jnp.maximum(m, jnp.max(s, axis=0, keepdims=True))
            alpha = jnp.exp(m - m_new)
            p = jnp.exp(s - m_new)
            l = alpha * l + jnp.sum(p, axis=0, keepdims=True)
            acc = alpha * acc + jnp.dot(vt[hh * HEAD_DIM:(hh + 1) * HEAD_DIM, :], p.astype(BF16),
                                        preferred_element_type=F32)
            out.append((m_new, l, acc))
        return tuple(out)

    init = tuple((jnp.full((1, MOBA_BLOCK), -jnp.inf, F32), jnp.zeros((1, MOBA_BLOCK), F32),
                  jnp.zeros((HEAD_DIM, MOBA_BLOCK), F32)) for _ in range(2))
    carry = lax.fori_loop(0, j, lambda n, c: step(n, c, False), init)
    carry = step(j, carry, True)
    outs = []
    for hh in range(2):
        _, l, acc = carry[hh]
        o = acc / l
        outs.append(o * lax.rsqrt(jnp.mean(o * o, axis=0, keepdims=True) + EPS))
    on = jnp.concatenate(outs, axis=0) * g_ref[...]
    o_ref[0] = on.T.astype(o_ref.dtype)


def _attn_prompt(qT, selT, cq, kbf, vT, kb, g_col):
    B, _, S = qT.shape
    nb = S // MOBA_BLOCK
    nbp = selT.shape[2]
    return pl.pallas_call(
        _attn_prompt_kernel,
        grid=(B, N_HEADS // 2, nb),
        in_specs=[
            pl.BlockSpec((1, LANES, MOBA_BLOCK), lambda b, p, j: (b, p, j)),
            pl.BlockSpec((1, 2, nbp, MOBA_BLOCK), lambda b, p, j: (b, p, 0, j)),
            pl.BlockSpec((2, LANES - nbp, MOBA_BLOCK), lambda b, p, j: (p, 0, 0)),
            pl.BlockSpec((1, S, LANES), lambda b, p, j: (b, 0, p)),
            pl.BlockSpec((1, nb, LANES, MOBA_BLOCK), lambda b, p, j: (b, 0, p, 0)),
            _const_spec((S, LANES)),
            pl.BlockSpec((LANES, 1), lambda b, p, j: (p, 0)),
        ],
        out_specs=pl.BlockSpec((1, MOBA_BLOCK, LANES), lambda b, p, j: (b, j, p)),
        out_shape=jax.ShapeDtypeStruct((B, S, ATTN_DIM), BF16),
        compiler_params=pltpu.CompilerParams(dimension_semantics=("arbitrary",) * 3,
                                             vmem_limit_bytes=VMEM_LIMIT),
        name="attn_prompt",
    )(qT, selT, cq, kbf, vT, kb, g_col)


def _conv_branch_out(c, lng, lnb, gco, gmat):
    mu = jnp.mean(c, axis=-1, keepdims=True)
    d = c - mu
    y = d * lax.rsqrt(jnp.mean(d * d, axis=-1, keepdims=True) + EPS) * lng + lnb
    y = y * jax.nn.sigmoid(y)
    gms = _split_dot(y * y, gmat)
    return (y * lax.rsqrt(gms + EPS) * gco).astype(BF16)


def _merge_tail(x, attn_n, conv_n, wout_ref, gpost):
    o = (jnp.dot(attn_n, wout_ref[0:ATTN_DIM, :], preferred_element_type=F32)
         + jnp.dot(conv_n, wout_ref[ATTN_DIM:, :], preferred_element_type=F32))
    return x + _rms(o, gpost)


def _merge_prompt_kernel(x_ref, attn_ref, u_ref, uh_ref, wdw_ref, bdw_ref, lng_ref, lnb_ref, gco_ref,
                         gmat_ref, wout_ref, gpost_ref, o_ref, ubuf, cbuf, *, rows, chunk):
    i = pl.program_id(1)
    ubuf[0:HALO, :] = jnp.where(i > 0, uh_ref[0], 0.0)
    ubuf[HALO:, :] = u_ref[0]
    first = HALO - (CONV_W - 1)
    for r0 in range(0, rows, chunk):
        acc = jnp.broadcast_to(bdw_ref[...], (chunk, 512))
        for w in range(CONV_W):
            acc = acc + wdw_ref[w:w + 1, :] * ubuf[first + w + r0:first + w + r0 + chunk, :]
        cbuf[r0:r0 + chunk, :] = acc
    conv_n = _conv_branch_out(cbuf[...], lng_ref[...], lnb_ref[...], gco_ref[...], gmat_ref[...])
    o_ref[0] = _merge_tail(x_ref[0], attn_ref[0], conv_n, wout_ref, gpost_ref[...])


def _merge_prompt(x, attn_n, u, wdw, bdw, lng, lnb, gco, gmat, wout_bf16, gpost, rows=256, chunk=32):
    B, S, D = x.shape
    tile = lambda b, i: (b, i, 0)
    halo_blocks = rows // HALO
    return pl.pallas_call(
        functools.partial(_merge_prompt_kernel, rows=rows, chunk=chunk),
        grid=(B, S // rows),
        in_specs=[
            pl.BlockSpec((1, rows, D), tile),
            pl.BlockSpec((1, rows, ATTN_DIM), tile),
            pl.BlockSpec((1, rows, 512), tile),
            pl.BlockSpec((1, HALO, 512), lambda b, i: (b, jnp.maximum(i * halo_blocks - 1, 0), 0)),
            _const_spec(wdw.shape), _const_spec((1, 512)), _const_spec((1, 512)), _const_spec((1, 512)),
            _const_spec((1, 512)), _const_spec((512, 512)), _const_spec(wout_bf16.shape), _const_spec((1, D)),
        ],
        out_specs=pl.BlockSpec((1, rows, D), tile),
        out_shape=jax.ShapeDtypeStruct((B, S, D), F32),
        scratch_shapes=[pltpu.VMEM((HALO + rows, 512), F32), pltpu.VMEM((rows, 512), F32)],
        compiler_params=pltpu.CompilerParams(dimension_semantics=("arbitrary", "arbitrary"),
                                             vmem_limit_bytes=VMEM_LIMIT),
        name="merge_prompt",
    )(x, attn_n, u, u, wdw, bdw, lng, lnb, gco, gmat, wout_bf16, gpost)


def _merge_sample_kernel(x_ref, attn_ref, ctx_ref, wdw_ref, bdw_ref, lng_ref, lnb_ref, gco_ref,
                         gmat_ref, wout_ref, gpost_ref, o_ref, cbuf, *, n_seq, t_new):
    def one_seq(s, _):
        acc = jnp.broadcast_to(bdw_ref[...], (t_new, 512))
        for w in range(CONV_W):
            acc = acc + wdw_ref[w:w + 1, :] * ctx_ref[s, w:w + t_new, :]
        cbuf[pl.ds(pl.multiple_of(s * t_new, t_new), t_new), :] = acc
        return 0

    lax.fori_loop(0, n_seq, one_seq, 0)
    conv_n = _conv_branch_out(cbuf[...], lng_ref[...], lnb_ref[...], gco_ref[...], gmat_ref[...])
    o_ref[...] = _merge_tail(x_ref[...], attn_ref[...], conv_n, wout_ref, gpost_ref[...])


def _merge_sample(x2d, attn_n, ctx, wdw, bdw, lng, lnb, gco, gmat, wout_bf16, gpost, n_seq=32):
    DB, ctx_len, _ = ctx.shape
    n_seq = min(n_seq, DB)
    t_new = ctx_len - (CONV_W - 1)
    D = x2d.shape[1]
    rows = n_seq * t_new
    tile = lambda i: (i, 0)
    return pl.pallas_call(
        functools.partial(_merge_sample_kernel, n_seq=n_seq, t_new=t_new),
        grid=(DB // n_seq,),
        in_specs=[
            pl.BlockSpec((rows, D), tile),
            pl.BlockSpec((rows, ATTN_DIM), tile),
            pl.BlockSpec((n_seq, ctx_len, 512), lambda i: (i, 0, 0)),
            _const_spec(wdw.shape), _const_spec((1, 512)), _const_spec((1, 512)), _const_spec((1, 512)),
            _const_spec((1, 512)), _const_spec((512, 512)), _const_spec(wout_bf16.shape), _const_spec((1, D)),
        ],
        out_specs=pl.BlockSpec((rows, D), tile),
        out_shape=jax.ShapeDtypeStruct(x2d.shape, F32),
        scratch_shapes=[pltpu.VMEM((rows, 512), F32)],
        compiler_params=pltpu.CompilerParams(dimension_semantics=("arbitrary",),
                                             vmem_limit_bytes=VMEM_LIMIT),
        name="merge_sample",
    )(x2d, attn_n, ctx, wdw, bdw, lng, lnb, gco, gmat, wout_bf16, gpost)


def _gated(c_gate, c_val):
    return (jax.nn.gelu(c_gate, approximate=True) * c_val).astype(BF16)


def _ffn_prompt_kernel(x_ref, gpre_ref, wup_ref, wdw_ref, bdw_ref, wdown_ref, gpost_ref, o_ref, st_ref,
                       upbuf, fbuf, *, rows, d_ff, chunk):
    i = pl.program_id(1)

    @pl.when(i == 0)
    def _():
        upbuf[0:8, :] = jnp.zeros((8, 2 * d_ff), F32)

    @pl.when(i > 0)
    def _():
        upbuf[0:8, :] = upbuf[rows:rows + 8, :]

    x = x_ref[0]
    h = _rms(x, gpre_ref[...]).astype(BF16)
    upbuf[8:, :] = jnp.dot(h, wup_ref[...], preferred_element_type=F32)
    st_ref[0] = upbuf[rows + 8 - (FFN_CONV_W - 1):rows + 8, :]
    first = 8 - (FFN_CONV_W - 1)
    for c0 in range(0, d_ff, chunk):
        halves = []
        for base in (c0, d_ff + c0):
            acc = jnp.broadcast_to(bdw_ref[:, base:base + chunk], (rows, chunk))
            for w in range(FFN_CONV_W):
                acc = acc + wdw_ref[w:w + 1, base:base + chunk] * upbuf[first + w:first + w + rows, base:base + chunk]
            halves.append(acc)
        fbuf[:, c0:c0 + chunk] = _gated(*halves)
    f = jnp.dot(fbuf[...], wdown_ref[...], preferred_element_type=F32)
    o_ref[0] = x + _rms(f, gpost_ref[...])


def _ffn_prompt(x, gpre, wup_bf16, wdw, bdw, wdown_bf16, gpost, rows=256, chunk=256):
    B, S, D = x.shape
    d_ff = wdown_bf16.shape[0]
    tile = lambda b, i: (b, i, 0)
    return pl.pallas_call(
        functools.partial(_ffn_prompt_kernel, rows=rows, d_ff=d_ff, chunk=chunk),
        grid=(B, S // rows),
        in_specs=[
            pl.BlockSpec((1, rows, D), tile), _const_spec((1, D)), _const_spec(wup_bf16.shape),
            _const_spec(wdw.shape), _const_spec((1, 2 * d_ff)), _const_spec(wdown_bf16.shape),
            _const_spec((1, D)),
        ],
        out_specs=[pl.BlockSpec((1, rows, D), tile),
                   pl.BlockSpec((1, FFN_CONV_W - 1, 2 * d_ff), lambda b, i: (b, 0, 0))],
        out_shape=[jax.ShapeDtypeStruct((B, S, D), F32),
                   jax.ShapeDtypeStruct((B, FFN_CONV_W - 1, 2 * d_ff), F32)],
        scratch_shapes=[pltpu.VMEM((8 + rows, 2 * d_ff), F32), pltpu.VMEM((rows, d_ff), BF16)],
        compiler_params=pltpu.CompilerParams(dimension_semantics=("arbitrary", "arbitrary"),
                                             vmem_limit_bytes=VMEM_LIMIT),
        name="ffn_prompt",
    )(x, gpre, wup_bf16, wdw, bdw, wdown_bf16, gpost)


def _ffn_sample_kernel(x_ref, st_in_ref, gpre_ref, wup_ref, wdw_ref, bdw_ref, wdown_ref, gpost_ref,
                       o_ref, st_ref, upbuf, fbuf, *, n_seq, t_new, d_ff, chunk):
    x = x_ref[...]
    h = _rms(x, gpre_ref[...]).astype(BF16)
    up = jnp.dot(h, wup_ref[...], preferred_element_type=F32)
    first = 8 - (FFN_CONV_W - 1)
    upbuf[:, first:8, :] = st_in_ref[...]
    upbuf[:, 8:, :] = up.reshape(n_seq, t_new, 2 * d_ff)
    st_ref[...] = upbuf[:, 8 + t_new - (FFN_CONV_W - 1):8 + t_new, :]
    for c0 in range(0, d_ff, chunk):
        halves = []
        for base in (c0, d_ff + c0):
            acc = jnp.broadcast_to(bdw_ref[:, base:base + chunk].reshape(1, 1, chunk), (n_seq, t_new, chunk))
            for w in range(FFN_CONV_W):
                acc = acc + (wdw_ref[w:w + 1, base:base + chunk].reshape(1, 1, chunk)
                             * upbuf[:, first + w:first + w + t_new, base:base + chunk])
            halves.append(acc.reshape(n_seq * t_new, chunk))
        fbuf[:, c0:c0 + chunk] = _gated(*halves)
    f = jnp.dot(fbuf[...], wdown_ref[...], preferred_element_type=F32)
    o_ref[...] = x + _rms(f, gpost_ref[...])


def _ffn_sample(x2d, st_in, gpre, wup_bf16, wdw, bdw, wdown_bf16, gpost, n_seq=16, chunk=256):
    DB = st_in.shape[0]
    n_seq = min(n_seq, DB)
    D = x2d.shape[1]
    t_new = x2d.shape[0] // DB
    d_ff = wdown_bf16.shape[0]
    rows = n_seq * t_new
    return pl.pallas_call(
        functools.partial(_ffn_sample_kernel, n_seq=n_seq, t_new=t_new, d_ff=d_ff, chunk=chunk),
        grid=(DB // n_seq,),
        in_specs=[
            pl.BlockSpec((rows, D), lambda i: (i, 0)),
            pl.BlockSpec((n_seq, FFN_CONV_W - 1, 2 * d_ff), lambda i: (i, 0, 0)),
            _const_spec((1, D)), _const_spec(wup_bf16.shape), _const_spec(wdw.shape),
            _const_spec((1, 2 * d_ff)), _const_spec(wdown_bf16.shape), _const_spec((1, D)),
        ],
        out_specs=[pl.BlockSpec((rows, D), lambda i: (i, 0)),
                   pl.BlockSpec((n_seq, FFN_CONV_W - 1, 2 * d_ff), lambda i: (i, 0, 0))],
        out_shape=[jax.ShapeDtypeStruct(x2d.shape, F32),
                   jax.ShapeDtypeStruct((DB, FFN_CONV_W - 1, 2 * d_ff), F32)],
        scratch_shapes=[pltpu.VMEM((n_seq, 8 + t_new, 2 * d_ff), F32), pltpu.VMEM((rows, d_ff), BF16)],
        compiler_params=pltpu.CompilerParams(dimension_semantics=("arbitrary",),
                                             vmem_limit_bytes=VMEM_LIMIT),
        name="ffn_sample",
    )(x2d, st_in, gpre, wup_bf16, wdw, bdw, wdown_bf16, gpost)


def _mix_sample_kernel(x_ref, g_ref, w_ref, q_ref, k_ref, v_ref, u_ref):
    h = _rms(x_ref[...], g_ref[...]).astype(BF16)
    z = jnp.dot(h, w_ref[...], preferred_element_type=F32)
    q_ref[...] = z[:, 0:ATTN_DIM] * (HEAD_DIM ** -0.5)
    k_ref[...] = z[:, ATTN_DIM:2 * ATTN_DIM]
    v_ref[...] = z[:, 2 * ATTN_DIM:3 * ATTN_DIM]
    u_ref[...] = z[:, 3 * ATTN_DIM:3 * ATTN_DIM + 512] * jax.nn.sigmoid(z[:, 3 * ATTN_DIM + 512:])


def _mix_sample(x2d, g, w_bf16, rows=256):
    N, D = x2d.shape
    rows = min(rows, N)
    tile = lambda i: (i, 0)
    out = jax.ShapeDtypeStruct((N, ATTN_DIM), F32)
    return pl.pallas_call(
        _mix_sample_kernel,
        grid=(N // rows,),
        in_specs=[pl.BlockSpec((rows, D), tile), _const_spec((1, D)), _const_spec(w_bf16.shape)],
        out_specs=[pl.BlockSpec((rows, ATTN_DIM), tile)] * 4,
        out_shape=[out] * 4,
        compiler_params=pltpu.CompilerParams(dimension_semantics=("arbitrary",),
                                             vmem_limit_bytes=VMEM_LIMIT),
        name="mix_sample",
    )(x2d, g, w_bf16)


def _attn_sample_kernel(pt_ref, q_ref, kn_ref, vn_ref, ab_ref, e2_ref, g_ref, *rest,
                        n_kv_steps, past, t_new):
    pages = PAGES_PER_STEP
    k_pages = rest[:pages]
    v_pages = rest[pages:2 * pages]
    o_ref = rest[2 * pages]
    logit_buf, p_buf, km_buf, acc_buf, stat_buf = rest[2 * pages + 1:]
    i = pl.program_id(1)
    n_col = N_HEADS * t_new
    page_size = k_pages[0].shape[1]
    blocks_per_step = pages * page_size // MOBA_BLOCK

    q = q_ref[0]
    row_head = lax.broadcasted_iota(jnp.int32, (n_col, ATTN_DIM), 0) // t_new
    lane_head = lax.broadcasted_iota(jnp.int32, (n_col, ATTN_DIM), 1) // HEAD_DIM
    own = row_head == lane_head
    qbd = jnp.where(own, jnp.concatenate([q] * N_HEADS, axis=0), 0.0)

    @pl.when(i < n_kv_steps)
    def _():
        qb = qbd.astype(BF16)
        sums = []
        for p in range(pages):
            kp = k_pages[p][0]
            logit_buf[i * pages + p] = lax.dot_general(
                qb, kp.astype(BF16), (((1,), (1,)), ((), ())), preferred_element_type=F32)
            sums.append(jnp.sum(kp, axis=0, keepdims=True))
        per_block = pages // blocks_per_step
        km_buf[i] = jnp.concatenate(
            [sum(sums[b * per_block:(b + 1) * per_block]) for b in range(blocks_per_step)], axis=0) * (1.0 / MOBA_BLOCK)

    @pl.when(i == n_kv_steps - 1)
    def _():
        nb = n_kv_steps * blocks_per_step
        km = jnp.concatenate([km_buf[...].reshape(nb, ATTN_DIM), jnp.zeros((LANES - nb, ATTN_DIM), F32)], axis=0)
        sc = lax.dot_general(qbd, km, (((1,), (1,)), ((), ())), preferred_element_type=F32,
                             precision=HIGHEST)
        ab = ab_ref[...]
        lane = lax.broadcasted_iota(jnp.int32, ab.shape, 1)
        coef = jnp.where(lane < nb, _top_k_bias(sc, lane < nb, 1), ab).astype(BF16)
        bias = jnp.dot(coef, e2_ref[...], preferred_element_type=F32)
        n_pages = n_kv_steps * pages
        kn = kn_ref[0].astype(BF16)
        l_own = lax.dot_general(qbd.astype(BF16), kn, (((1,), (1,)), ((), ())), preferred_element_type=F32)
        key_t = lax.broadcasted_iota(jnp.int32, (n_col, t_new), 1)
        qry_t = lax.broadcasted_iota(jnp.int32, (n_col, t_new), 0) % t_new
        slope = ab[:, nb:nb + 1]
        l_own = l_own + slope * (past + key_t).astype(F32)
        l_own = jnp.where(key_t <= qry_t, l_own, NEG)
        m = jnp.max(l_own, axis=1, keepdims=True)
        for pg in range(n_pages):
            lg = logit_buf[pg] + bias[:, pg * page_size:(pg + 1) * page_size]
            logit_buf[pg] = lg
            m = jnp.maximum(m, jnp.max(lg, axis=1, keepdims=True))
        p_own = jnp.exp(l_own - m)
        l = jnp.sum(p_own, axis=1, keepdims=True)
        for pg in range(n_pages):
            pe = jnp.exp(logit_buf[pg] - m)
            l = l + jnp.sum(pe, axis=1, keepdims=True)
            p_buf[pg] = pe.astype(BF16)
        stat_buf[...] = jnp.broadcast_to(l, stat_buf.shape)
        acc_buf[...] = jnp.dot(p_own.astype(BF16), vn_ref[0].astype(BF16), preferred_element_type=F32)

    @pl.when(i >= n_kv_steps)
    def _():
        acc = acc_buf[...]
        for p in range(pages):
            acc = acc + jnp.dot(p_buf[(i - n_kv_steps) * pages + p], v_pages[p][0].astype(BF16),
                                preferred_element_type=F32)
        acc_buf[...] = acc

    @pl.when(i == 2 * n_kv_steps - 1)
    def _():
        o = acc_buf[...] / stat_buf[:, 0:1]
        o = jnp.where(own, o, 0.0)
        on = o * lax.rsqrt(jnp.sum(o * o, axis=1, keepdims=True) * (1.0 / HEAD_DIM) + EPS)
        out = on[0:t_new]
        for hh in range(1, N_HEADS):
            out = out + on[hh * t_new:(hh + 1) * t_new]
        o_ref[0] = (out * g_ref[...]).astype(o_ref.dtype)


def _attn_sample(page_table, q, kn, vn, cache_k, cache_v, ab, e2, g_attn):
    DB, t_new, _ = q.shape
    n_pages = page_table.shape[1]
    page_size = cache_k.shape[1]
    past = n_pages * page_size
    pages = PAGES_PER_STEP
    n_kv_steps = n_pages // pages
    n_col = N_HEADS * t_new

    def k_map(p):
        def index(b, i, pt):
            return (pt[b * n_pages + jnp.minimum(i, n_kv_steps - 1) * pages + p], 0, 0)
        return index

    def v_map(p):
        def index(b, i, pt):
            in_v = i >= n_kv_steps
            seq = jnp.where(in_v, b, jnp.maximum(b - 1, 0))
            step = jnp.where(in_v, i - n_kv_steps, jnp.where(b > 0, n_kv_steps - 1, 0))
            return (pt[seq * n_pages + step * pages + p], 0, 0)
        return index

    per_seq = lambda b, i, pt: (b, 0, 0)
    page_block = (1, page_size, ATTN_DIM)
    grid_spec = pltpu.PrefetchScalarGridSpec(
        num_scalar_prefetch=1,
        grid=(DB, 2 * n_kv_steps),
        in_specs=[
            pl.BlockSpec((1, t_new, ATTN_DIM), per_seq),
            pl.BlockSpec((1, t_new, ATTN_DIM), per_seq),
            pl.BlockSpec((1, t_new, ATTN_DIM), per_seq),
            _const_spec(ab.shape), _const_spec(e2.shape), _const_spec((1, ATTN_DIM)),
        ] + [pl.BlockSpec(page_block, k_map(p)) for p in range(pages)]
          + [pl.BlockSpec(page_block, v_map(p)) for p in range(pages)],
        out_specs=pl.BlockSpec((1, t_new, ATTN_DIM), per_seq),
        scratch_shapes=[
            pltpu.VMEM((n_pages, n_col, page_size), F32),
            pltpu.VMEM((n_pages, n_col, page_size), BF16),
            pltpu.VMEM((n_kv_steps, pages * page_size // MOBA_BLOCK, ATTN_DIM), F32),
            pltpu.VMEM((n_col, ATTN_DIM), F32),
            pltpu.VMEM((n_col, LANES), F32),
        ],
    )
    return pl.pallas_call(
        functools.partial(_attn_sample_kernel, n_kv_steps=n_kv_steps, past=past, t_new=t_new),
        grid_spec=grid_spec,
        out_shape=jax.ShapeDtypeStruct((DB, t_new, ATTN_DIM), BF16),
        compiler_params=pltpu.CompilerParams(dimension_semantics=("arbitrary", "arbitrary"),
                                             vmem_limit_bytes=VMEM_LIMIT),
        name="attn_sample",
    )(page_table.reshape(-1), q, kn, vn, ab, e2, g_attn, *([cache_k] * pages), *([cache_v] * pages))


def _alibi_slopes():
    return jnp.exp2(-8.0 * jnp.arange(1, N_HEADS + 1, dtype=F32) / N_HEADS)


def _position_tables(n_keys, nbp, t_new):
    slopes = _alibi_slopes()
    pos = jnp.arange(n_keys)
    blk, off = pos // MOBA_BLOCK, pos % MOBA_BLOCK
    lane = jnp.arange(LANES)
    kb = jnp.where(lane[None, :] == blk[:, None], 1.0, 0.0)
    kb = jnp.where(lane[None, :] == nbp, off[:, None].astype(F32), kb)
    kb = jnp.where(lane[None, :] == nbp + 1, blk[:, None].astype(F32), kb)
    row = jnp.arange(LANES - nbp)
    coef = jnp.where(row == 0, 1.0, jnp.where(row == 1, float(MOBA_BLOCK), 0.0))
    cq = slopes[:, None, None] * jnp.broadcast_to(coef[None, :, None], (N_HEADS, LANES - nbp, MOBA_BLOCK))
    ab = jnp.zeros((N_HEADS * t_new, LANES), F32)
    ab = ab.at[:, nbp].set(jnp.repeat(slopes, t_new)).at[:, nbp + 1].set(jnp.repeat(slopes, t_new) * MOBA_BLOCK)
    return kb.astype(BF16), cq.astype(BF16), ab


def kernel(x_prompt, x_sample, cache_k, cache_v, page_table, state_conv, state_ffn_conv, g_pre_mix, w_in,
           w_conv_dw, b_conv_dw, g_conv_ln, b_conv_ln, g_attn_out, g_conv_out, w_out, g_post_mix, g_pre_ffn,
           w_up, w_ffn_dw, b_ffn_dw, w_down, g_post_ffn):
    depth = w_in.shape[0]
    B, S, D = x_prompt.shape
    DB, T, _ = x_sample.shape
    n_pool, page_size = cache_k.shape[1], cache_k.shape[2]
    n_pages = page_table.shape[1]
    past = n_pages * page_size
    assert S % MOBA_BLOCK == 0 and S >= CONV_W - 1
    assert past % MOBA_BLOCK == 0 and past // MOBA_BLOCK >= MOBA_TOPK, "the cached past must be whole key blocks"
    assert n_pages % PAGES_PER_STEP == 0 and (PAGES_PER_STEP * page_size) % MOBA_BLOCK == 0
    assert T <= MOBA_BLOCK
    nb_p = S // MOBA_BLOCK
    nbp_p = _round_up(nb_p, BF16_SUBLANES)
    nb_s = past // MOBA_BLOCK
    assert nbp_p + 2 <= LANES and nb_s + 2 <= LANES

    kb_p, cq_p, _ = _position_tables(S, nbp_p, T)
    kb_s, _, ab_s = _position_tables(past, nb_s, T)
    e2_s = kb_s.T
    group = jnp.arange(512) // (512 // CONV_GROUPS)
    gmat = jnp.where(group[:, None] == group[None, :], 1.0 / (512 // CONV_GROUPS), 0.0).astype(BF16)

    xp, xs = x_prompt, x_sample.reshape(DB * T, D)
    row = lambda a: a.reshape(1, -1)
    outs = {k: [] for k in ("kp", "vp", "cp", "fp", "ks", "vs", "cs", "fs")}
    for l in range(depth):
        w_in_b, w_out_b = w_in[l].astype(BF16), w_out[l].astype(BF16)
        w_up_b, w_down_b = w_up[l].astype(BF16), w_down[l].astype(BF16)
        wdw = jnp.pad(w_conv_dw[l], ((0, HALO - CONV_W), (0, 0)))
        wfd = jnp.pad(w_ffn_dw[l], ((0, 8 - FFN_CONV_W), (0, 0)))
        conv_args = (wdw, row(b_conv_dw[l]), row(g_conv_ln[l]), row(b_conv_ln[l]), row(g_conv_out[l]), gmat,
                     w_out_b, row(g_post_mix[l]))
        ffn_args = (row(g_pre_ffn[l]), w_up_b, wfd, row(b_ffn_dw[l]), w_down_b, row(g_post_ffn[l]))

        qT, selT, kbf, vT, kp, vp, u = _mix_prompt(xp, row(g_pre_mix[l]), w_in_b, nbp_p)
        attn = _attn_prompt(qT, selT, cq_p, kbf, vT, kb_p, g_attn_out[l].reshape(-1, 1))
        xp = _merge_prompt(xp, attn, u, *conv_args)
        xp, fst_p = _ffn_prompt(xp, *ffn_args)
        outs["kp"].append(kp.reshape(B, S, N_HEADS, HEAD_DIM))
        outs["vp"].append(vp.reshape(B, S, N_HEADS, HEAD_DIM))
        outs["cp"].append(u[:, S - (CONV_W - 1):])
        outs["fp"].append(fst_p)

        q_s, k_s, v_s, u_s = _mix_sample(xs, row(g_pre_mix[l]), w_in_b)
        as3 = lambda a: a.reshape(DB, T, ATTN_DIM)
        attn_s = _attn_sample(page_table, as3(q_s), as3(k_s), as3(v_s),
                              cache_k[l].reshape(n_pool, page_size, ATTN_DIM),
                              cache_v[l].reshape(n_pool, page_size, ATTN_DIM), ab_s, e2_s, row(g_attn_out[l]))
        ctx = jnp.concatenate([state_conv[l], u_s.reshape(DB, T, 512)], axis=1)
        xs = _merge_sample(xs, attn_s.reshape(DB * T, ATTN_DIM), ctx, *conv_args)
        xs, fst_s = _ffn_sample(xs, state_ffn_conv[l], *ffn_args)
        outs["ks"].append(k_s.reshape(DB, T, N_HEADS, HEAD_DIM))
        outs["vs"].append(v_s.reshape(DB, T, N_HEADS, HEAD_DIM))
        outs["cs"].append(ctx[:, T:])
        outs["fs"].append(fst_s)

    st = lambda name: jnp.stack(outs[name])
    return (xp, xs.reshape(DB, T, D), st("kp"), st("vp"), st("cp"), st("fp"),
            st("ks"), st("vs"), st("cs"), st("fs"))
```

```python
import functools

import jax
import jax.numpy as jnp
from jax import lax
from jax.experimental import pallas as pl
from jax.experimental.pallas import tpu as pltpu

N_HEADS = 8
HEAD_DIM = 64
ATTN_DIM = N_HEADS * HEAD_DIM
CONV_GROUPS = 8
CONV_W = 31
MOBA_BLOCK = 256
MOBA_TOPK = 3
FFN_CONV_W = 3
EPS = 1e-6
NEG = -1e30
LOG2E = 1.4426950408889634

LANES = 128
BF16_SUBLANES = 16
HALO = 32
PAGES_PER_STEP = 16
VMEM_LIMIT = 56 * 1024 * 1024

F32 = jnp.float32
BF16 = jnp.bfloat16
HIGHEST = lax.Precision.HIGHEST


def _round_up(n, m):
    return -(-n // m) * m


def _const_spec(shape):
    zeros = (0,) * len(shape)
    return pl.BlockSpec(shape, lambda *_: zeros)


def _rms(x, g):
    return x * lax.rsqrt(jnp.mean(x * x, axis=-1, keepdims=True) + EPS) * g


def _split_dot(x, w_bf16):
    hi = x.astype(BF16)
    lo = (x - hi.astype(F32)).astype(BF16)
    return (jnp.dot(hi, w_bf16, preferred_element_type=F32)
            + jnp.dot(lo, w_bf16, preferred_element_type=F32))


def _top_k_bias(sc, cand, axis):
    n = sc.shape[axis]
    idx = lax.broadcasted_iota(jnp.int32, sc.shape, axis)
    s = jnp.where(cand, sc, -jnp.inf)
    bias = jnp.full(sc.shape, NEG, F32)
    for _ in range(MOBA_TOPK):
        m = jnp.max(s, axis=axis, keepdims=True)
        first = jnp.min(jnp.where(s == m, idx, n), axis=axis, keepdims=True)
        pick = idx == first
        bias = jnp.where(pick, 0.0, bias)
        s = jnp.where(pick, -jnp.inf, s)
    return jnp.where(cand, bias, NEG)


def _mix_prompt_kernel(x_ref, g_ref, w_ref, qT_ref, selT_ref, kbf_ref, vT_ref, k_ref, v_ref, u_ref,
                       km_ref):
    j = pl.program_id(1)
    nbp = km_ref.shape[0]

    @pl.when(j == 0)
    def _():
        km_ref[...] = jnp.zeros_like(km_ref)

    h = _rms(x_ref[0], g_ref[...]).astype(BF16)
    z = jnp.dot(h, w_ref[...], preferred_element_type=F32)
    q = z[:, 0:ATTN_DIM] * (HEAD_DIM ** -0.5 * LOG2E)
    k = z[:, ATTN_DIM:2 * ATTN_DIM]
    v = z[:, 2 * ATTN_DIM:3 * ATTN_DIM]
    a = z[:, 3 * ATTN_DIM:3 * ATTN_DIM + 512]
    gg = z[:, 3 * ATTN_DIM + 512:]
    k_ref[0] = k
    v_ref[0] = v
    kbf_ref[0] = k.astype(BF16)
    u_ref[0] = a * jax.nn.sigmoid(gg)
    qT = q.T
    qT_ref[0] = qT.astype(BF16)
    vT_ref[0, 0] = v.T.astype(BF16)

    km = km_ref[...]
    blk = lax.broadcasted_iota(jnp.int32, (nbp, MOBA_BLOCK), 0)
    lane_head = lax.broadcasted_iota(jnp.int32, (nbp, LANES), 1) // HEAD_DIM
    for hh in range(N_HEADS):
        pr = hh // 2
        kmp = jnp.where(lane_head == hh % 2, km[:, pr * LANES:(pr + 1) * LANES], 0.0)
        sc = jnp.dot(kmp, qT[pr * LANES:(pr + 1) * LANES, :], preferred_element_type=F32,
                     precision=HIGHEST)
        selT_ref[0, hh] = _top_k_bias(sc, blk < j, 0).astype(BF16)

    row = lax.broadcasted_iota(jnp.int32, km.shape, 0)
    km_ref[...] = jnp.where(row == j, jnp.mean(k, axis=0, keepdims=True), km)


def _mix_prompt(x, g, w_bf16, nbp):
    B, S, D = x.shape
    nb = S // MOBA_BLOCK
    n_out = w_bf16.shape[1]
    tile = lambda b, j: (b, j, 0)
    return pl.pallas_call(
        _mix_prompt_kernel,
        grid=(B, nb),
        in_specs=[pl.BlockSpec((1, MOBA_BLOCK, D), tile), _const_spec((1, D)), _const_spec((D, n_out))],
        out_specs=[
            pl.BlockSpec((1, ATTN_DIM, MOBA_BLOCK), lambda b, j: (b, 0, j)),
            pl.BlockSpec((1, N_HEADS, nbp, MOBA_BLOCK), lambda b, j: (b, 0, 0, j)),
            pl.BlockSpec((1, MOBA_BLOCK, ATTN_DIM), tile),
            pl.BlockSpec((1, 1, ATTN_DIM, MOBA_BLOCK), lambda b, j: (b, j, 0, 0)),
            pl.BlockSpec((1, MOBA_BLOCK, ATTN_DIM), tile),
            pl.BlockSpec((1, MOBA_BLOCK, ATTN_DIM), tile),
            pl.BlockSpec((1, MOBA_BLOCK, 512), tile),
        ],
        out_shape=[
            jax.ShapeDtypeStruct((B, ATTN_DIM, S), BF16),
            jax.ShapeDtypeStruct((B, N_HEADS, nbp, S), BF16),
            jax.ShapeDtypeStruct((B, S, ATTN_DIM), BF16),
            jax.ShapeDtypeStruct((B, nb, ATTN_DIM, MOBA_BLOCK), BF16),
            jax.ShapeDtypeStruct((B, S, ATTN_DIM), F32),
            jax.ShapeDtypeStruct((B, S, ATTN_DIM), F32),
            jax.ShapeDtypeStruct((B, S, 512), F32),
        ],
        scratch_shapes=[pltpu.VMEM((nbp, ATTN_DIM), F32)],
        compiler_params=pltpu.CompilerParams(dimension_semantics=("arbitrary", "arbitrary"),
                                             vmem_limit_bytes=VMEM_LIMIT),
        name="mix_prompt",
    )(x, g, w_bf16)


def _attn_prompt_kernel(qT_ref, selT_ref, cq_ref, k_ref, vT_ref, kb_ref, g_ref, o_ref, s_buf, p_buf):
    j = pl.program_id(2)
    nb = vT_ref.shape[1]
    nbp = selT_ref.shape[2]
    qT = qT_ref[0]
    sub_head = lax.broadcasted_iota(jnp.int32, qT.shape, 0) // HEAD_DIM
    no_sel = jnp.zeros((nbp, MOBA_BLOCK), BF16)
    qq, qq_own = [], []
    for hh in range(2):
        qm = jnp.where(sub_head == hh, qT, jnp.zeros_like(qT))
        qq.append(jnp.concatenate([qm, selT_ref[0, hh], cq_ref[hh]], axis=0))
        qq_own.append(jnp.concatenate([qm, no_sel, cq_ref[hh]], axis=0))

    def stage_a(b, slot, rhs, causal):
        off = pl.multiple_of(b * MOBA_BLOCK, MOBA_BLOCK)
        kk = jnp.concatenate([k_ref[0, pl.ds(off, MOBA_BLOCK), :], kb_ref[pl.ds(off, MOBA_BLOCK), :]], axis=1)
        col_max = []
        for hh in range(2):
            s = jnp.dot(kk, rhs[hh], preferred_element_type=F32)
            if causal:
                key_i = lax.broadcasted_iota(jnp.int32, s.shape, 0)
                qry_i = lax.broadcasted_iota(jnp.int32, s.shape, 1)
                s = jnp.where(key_i <= qry_i, s, NEG)
            s_buf[slot, hh] = s
            col_max.append(jnp.max(s, axis=0, keepdims=True))
        return col_max

    def stage_b(slot, col_max, m):
        m_out, alpha_out = [], []
        for hh in range(2):
            m_new = jnp.maximum(m[hh], col_max[hh])
            p_buf[slot, hh] = jnp.exp2(s_buf[slot, hh] - m_new).astype(BF16)
            m_out.append(m_new)
            alpha_out.append(jnp.exp2(m[hh] - m_new))
        return m_out, alpha_out

    ones = jnp.ones((BF16_SUBLANES, MOBA_BLOCK), BF16)

    def stage_c(b, slot, alpha, acc):
        vt = vT_ref[0, b]
        return [alpha[hh] * acc[hh]
                + jnp.dot(jnp.concatenate([vt[hh * HEAD_DIM:(hh + 1) * HEAD_DIM, :], ones], axis=0),
                          p_buf[slot, hh], preferred_element_type=F32) for hh in range(2)]

    minus_inf = [jnp.full((1, MOBA_BLOCK), -jnp.inf, F32)] * 2
    m, alpha_c = stage_b(1, stage_a(j, 1, qq_own, True), minus_inf)
    cm0 = stage_a(0, 0, qq, False)
    acc = [jnp.zeros((HEAD_DIM + BF16_SUBLANES, MOBA_BLOCK), F32)] * 2

    def pair(t, carry):
        m, acc, cm0, alpha_c = carry
        b0 = 2 * t
        cm1 = stage_a(b0 + 1, 1, qq, False)
        m, alpha0 = stage_b(0, cm0, m)
        acc = stage_c(jnp.where(t == 0, j, b0 - 1), 1, alpha_c, acc)
        cm0 = stage_a(jnp.minimum(b0 + 2, nb - 1), 0, qq, False)
        m, alpha_c = stage_b(1, cm1, m)
        acc = stage_c(b0, 0, alpha0, acc)
        return m, acc, cm0, alpha_c

    n_pairs = (j + 1) // 2
    m, acc, cm0, alpha_c = lax.fori_loop(0, n_pairs, pair, (m, acc, cm0, alpha_c))
    acc = stage_c(jnp.where(n_pairs == 0, j, 2 * n_pairs - 1), 1, alpha_c, acc)

    outs = []
    for hh in range(2):
        o = acc[hh][0:HEAD_DIM] / acc[hh][HEAD_DIM:HEAD_DIM + 1]
        outs.append(o * lax.rsqrt(jnp.mean(o * o, axis=0, keepdims=True) + EPS))
    on = jnp.concatenate(outs, axis=0) * g_ref[...]
    o_ref[0] = on.T.astype(o_ref.dtype)


def _attn_prompt(qT, selT, cq, kbf, vT, kb, g_col):
    B, _, S = qT.shape
    nb = S // MOBA_BLOCK
    nbp = selT.shape[2]
    return pl.pallas_call(
        _attn_prompt_kernel,
        grid=(B, N_HEADS // 2, nb),
        in_specs=[
            pl.BlockSpec((1, LANES, MOBA_BLOCK), lambda b, p, j: (b, p, j)),
            pl.BlockSpec((1, 2, nbp, MOBA_BLOCK), lambda b, p, j: (b, p, 0, j)),
            pl.BlockSpec((2, LANES - nbp, MOBA_BLOCK), lambda b, p, j: (p, 0, 0)),
            pl.BlockSpec((1, S, LANES), lambda b, p, j: (b, 0, p)),
            pl.BlockSpec((1, nb, LANES, MOBA_BLOCK), lambda b, p, j: (b, 0, p, 0)),
            _const_spec((S, LANES)),
            pl.BlockSpec((LANES, 1), lambda b, p, j: (p, 0)),
        ],
        out_specs=pl.BlockSpec((1, MOBA_BLOCK, LANES), lambda b, p, j: (b, j, p)),
        out_shape=jax.ShapeDtypeStruct((B, S, ATTN_DIM), BF16),
        scratch_shapes=[pltpu.VMEM((2, 2, MOBA_BLOCK, MOBA_BLOCK), F32),
                        pltpu.VMEM((2, 2, MOBA_BLOCK, MOBA_BLOCK), BF16)],
        compiler_params=pltpu.CompilerParams(dimension_semantics=("arbitrary",) * 3,
                                             vmem_limit_bytes=VMEM_LIMIT),
        name="attn_prompt",
    )(qT, selT, cq, kbf, vT, kb, g_col)


def _conv_branch_out(c, lng, lnb, gco, gmat):
    mu = jnp.mean(c, axis=-1, keepdims=True)
    d = c - mu
    y = d * lax.rsqrt(jnp.mean(d * d, axis=-1, keepdims=True) + EPS) * lng + lnb
    y = y * jax.nn.sigmoid(y)
    gms = _split_dot(y * y, gmat)
    return (y * lax.rsqrt(gms + EPS) * gco).astype(BF16)


def _merge_tail(x, attn_n, conv_n, wout_ref, gpost):
    o = (jnp.dot(attn_n, wout_ref[0:ATTN_DIM, :], preferred_element_type=F32)
         + jnp.dot(conv_n, wout_ref[ATTN_DIM:, :], preferred_element_type=F32))
    return x + _rms(o, gpost)


def _merge_prompt_kernel(x_ref, attn_ref, u_ref, uh_ref, wdw_ref, bdw_ref, lng_ref, lnb_ref, gco_ref,
                         gmat_ref, wout_ref, gpost_ref, o_ref, ubuf, cbuf, *, rows, chunk):
    i = pl.program_id(1)
    ubuf[0:HALO, :] = jnp.where(i > 0, uh_ref[0], 0.0)
    ubuf[HALO:, :] = u_ref[0]
    first = HALO - (CONV_W - 1)
    for r0 in range(0, rows, chunk):
        acc = jnp.broadcast_to(bdw_ref[...], (chunk, 512))
        for w in range(CONV_W):
            acc = acc + wdw_ref[w:w + 1, :] * ubuf[first + w + r0:first + w + r0 + chunk, :]
        cbuf[r0:r0 + chunk, :] = acc
    conv_n = _conv_branch_out(cbuf[...], lng_ref[...], lnb_ref[...], gco_ref[...], gmat_ref[...])
    o_ref[0] = _merge_tail(x_ref[0], attn_ref[0], conv_n, wout_ref, gpost_ref[...])


def _merge_prompt(x, attn_n, u, wdw, bdw, lng, lnb, gco, gmat, wout_bf16, gpost, rows=256, chunk=32):
    B, S, D = x.shape
    tile = lambda b, i: (b, i, 0)
    halo_blocks = rows // HALO
    return pl.pallas_call(
        functools.partial(_merge_prompt_kernel, rows=rows, chunk=chunk),
        grid=(B, S // rows),
        in_specs=[
            pl.BlockSpec((1, rows, D), tile),
            pl.BlockSpec((1, rows, ATTN_DIM), tile),
            pl.BlockSpec((1, rows, 512), tile),
            pl.BlockSpec((1, HALO, 512), lambda b, i: (b, jnp.maximum(i * halo_blocks - 1, 0), 0)),
            _const_spec(wdw.shape), _const_spec((1, 512)), _const_spec((1, 512)), _const_spec((1, 512)),
            _const_spec((1, 512)), _const_spec((512, 512)), _const_spec(wout_bf16.shape), _const_spec((1, D)),
        ],
        out_specs=pl.BlockSpec((1, rows, D), tile),
        out_shape=jax.ShapeDtypeStruct((B, S, D), F32),
        scratch_shapes=[pltpu.VMEM((HALO + rows, 512), F32), pltpu.VMEM((rows, 512), F32)],
        compiler_params=pltpu.CompilerParams(dimension_semantics=("arbitrary", "arbitrary"),
                                             vmem_limit_bytes=VMEM_LIMIT),
        name="merge_prompt",
    )(x, attn_n, u, u, wdw, bdw, lng, lnb, gco, gmat, wout_bf16, gpost)


def _merge_sample_kernel(x_ref, attn_ref, ctx_ref, wdw_ref, bdw_ref, lng_ref, lnb_ref, gco_ref,
                         gmat_ref, wout_ref, gpost_ref, o_ref, cbuf, *, n_seq, t_new):
    def one_seq(s, _):
        acc = jnp.broadcast_to(bdw_ref[...], (t_new, 512))
        for w in range(CONV_W):
            acc = acc + wdw_ref[w:w + 1, :] * ctx_ref[s, w:w + t_new, :]
        cbuf[pl.ds(pl.multiple_of(s * t_new, t_new), t_new), :] = acc
        return 0

    lax.fori_loop(0, n_seq, one_seq, 0)
    conv_n = _conv_branch_out(cbuf[...], lng_ref[...], lnb_ref[...], gco_ref[...], gmat_ref[...])
    o_ref[...] = _merge_tail(x_ref[...], attn_ref[...], conv_n, wout_ref, gpost_ref[...])


def _merge_sample(x2d, attn_n, ctx, wdw, bdw, lng, lnb, gco, gmat, wout_bf16, gpost, n_seq=32):
    DB, ctx_len, _ = ctx.shape
    n_seq = min(n_seq, DB)
    t_new = ctx_len - (CONV_W - 1)
    D = x2d.shape[1]
    rows = n_seq * t_new
    tile = lambda i: (i, 0)
    return pl.pallas_call(
        functools.partial(_merge_sample_kernel, n_seq=n_seq, t_new=t_new),
        grid=(DB // n_seq,),
        in_specs=[
            pl.BlockSpec((rows, D), tile),
            pl.BlockSpec((rows, ATTN_DIM), tile),
            pl.BlockSpec((n_seq, ctx_len, 512), lambda i: (i, 0, 0)),
            _const_spec(wdw.shape), _const_spec((1, 512)), _const_spec((1, 512)), _const_spec((1, 512)),
            _const_spec((1, 512)), _const_spec((512, 512)), _const_spec(wout_bf16.shape), _const_spec((1, D)),
        ],
        out_specs=pl.BlockSpec((rows, D), tile),
        out_shape=jax.ShapeDtypeStruct(x2d.shape, F32),
        scratch_shapes=[pltpu.VMEM((rows, 512), F32)],
        compiler_params=pltpu.CompilerParams(dimension_semantics=("arbitrary",),
                                             vmem_limit_bytes=VMEM_LIMIT),
        name="merge_sample",
    )(x2d, attn_n, ctx, wdw, bdw, lng, lnb, gco, gmat, wout_bf16, gpost)


def _gated(c_gate, c_val):
    return (jax.nn.gelu(c_gate, approximate=True) * c_val).astype(BF16)


def _ffn_prompt_kernel(x_ref, gpre_ref, wup_ref, wdw_ref, bdw_ref, wdown_ref, gpost_ref, o_ref, st_ref,
                       upbuf, fbuf, *, rows, d_ff, chunk):
    i = pl.program_id(1)

    @pl.when(i == 0)
    def _():
        upbuf[0:8, :] = jnp.zeros((8, 2 * d_ff), F32)

    @pl.when(i > 0)
    def _():
        upbuf[0:8, :] = upbuf[rows:rows + 8, :]

    x = x_ref[0]
    h = _rms(x, gpre_ref[...]).astype(BF16)
    upbuf[8:, :] = jnp.dot(h, wup_ref[...], preferred_element_type=F32)
    st_ref[0] = upbuf[rows + 8 - (FFN_CONV_W - 1):rows + 8, :]
    first = 8 - (FFN_CONV_W - 1)
    for c0 in range(0, d_ff, chunk):
        halves = []
        for base in (c0, d_ff + c0):
            acc = jnp.broadcast_to(bdw_ref[:, base:base + chunk], (rows, chunk))
            for w in range(FFN_CONV_W):
                acc = acc + wdw_ref[w:w + 1, base:base + chunk] * upbuf[first + w:first + w + rows, base:base + chunk]
            halves.append(acc)
        fbuf[:, c0:c0 + chunk] = _gated(*halves)
    f = jnp.dot(fbuf[...], wdown_ref[...], preferred_element_type=F32)
    o_ref[0] = x + _rms(f, gpost_ref[...])


def _ffn_prompt(x, gpre, wup_bf16, wdw, bdw, wdown_bf16, gpost, rows=256, chunk=256):
    B, S, D = x.shape
    d_ff = wdown_bf16.shape[0]
    tile = lambda b, i: (b, i, 0)
    return pl.pallas_call(
        functools.partial(_ffn_prompt_kernel, rows=rows, d_ff=d_ff, chunk=chunk),
        grid=(B, S // rows),
        in_specs=[
            pl.BlockSpec((1, rows, D), tile), _const_spec((1, D)), _const_spec(wup_bf16.shape),
            _const_spec(wdw.shape), _const_spec((1, 2 * d_ff)), _const_spec(wdown_bf16.shape),
            _const_spec((1, D)),
        ],
        out_specs=[pl.BlockSpec((1, rows, D), tile),
                   pl.BlockSpec((1, FFN_CONV_W - 1, 2 * d_ff), lambda b, i: (b, 0, 0))],
        out_shape=[jax.ShapeDtypeStruct((B, S, D), F32),
                   jax.ShapeDtypeStruct((B, FFN_CONV_W - 1, 2 * d_ff), F32)],
        scratch_shapes=[pltpu.VMEM((8 + rows, 2 * d_ff), F32), pltpu.VMEM((rows, d_ff), BF16)],
        compiler_params=pltpu.CompilerParams(dimension_semantics=("arbitrary", "arbitrary"),
                                             vmem_limit_bytes=VMEM_LIMIT),
        name="ffn_prompt",
    )(x, gpre, wup_bf16, wdw, bdw, wdown_bf16, gpost)


def _ffn_sample_kernel(x_ref, st_in_ref, gpre_ref, wup_ref, wdw_ref, bdw_ref, wdown_ref, gpost_ref,
                       o_ref, st_ref, upbuf, fbuf, *, n_seq, t_new, d_ff, chunk):
    x = x_ref[...]
    h = _rms(x, gpre_ref[...]).astype(BF16)
    up = jnp.dot(h, wup_ref[...], preferred_element_type=F32)
    first = 8 - (FFN_CONV_W - 1)
    upbuf[:, first:8, :] = st_in_ref[...]
    upbuf[:, 8:, :] = up.reshape(n_seq, t_new, 2 * d_ff)
    st_ref[...] = upbuf[:, 8 + t_new - (FFN_CONV_W - 1):8 + t_new, :]
    for c0 in range(0, d_ff, chunk):
        halves = []
        for base in (c0, d_ff + c0):
            acc = jnp.broadcast_to(bdw_ref[:, base:base + chunk].reshape(1, 1, chunk), (n_seq, t_new, chunk))
            for w in range(FFN_CONV_W):
                acc = acc + (wdw_ref[w:w + 1, base:base + chunk].reshape(1, 1, chunk)
                             * upbuf[:, first + w:first + w + t_new, base:base + chunk])
            halves.append(acc.reshape(n_seq * t_new, chunk))
        fbuf[:, c0:c0 + chunk] = _gated(*halves)
    f = jnp.dot(fbuf[...], wdown_ref[...], preferred_element_type=F32)
    o_ref[...] = x + _rms(f, gpost_ref[...])


def _ffn_sample(x2d, st_in, gpre, wup_bf16, wdw, bdw, wdown_bf16, gpost, n_seq=16, chunk=256):
    DB = st_in.shape[0]
    n_seq = min(n_seq, DB)
    D = x2d.shape[1]
    t_new = x2d.shape[0] // DB
    d_ff = wdown_bf16.shape[0]
    rows = n_seq * t_new
    return pl.pallas_call(
        functools.partial(_ffn_sample_kernel, n_seq=n_seq, t_new=t_new, d_ff=d_ff, chunk=chunk),
        grid=(DB // n_seq,),
        in_specs=[
            pl.BlockSpec((rows, D), lambda i: (i, 0)),
            pl.BlockSpec((n_seq, FFN_CONV_W - 1, 2 * d_ff), lambda i: (i, 0, 0)),
            _const_spec((1, D)), _const_spec(wup_bf16.shape), _const_spec(wdw.shape),
            _const_spec((1, 2 * d_ff)), _const_spec(wdown_bf16.shape), _const_spec((1, D)),
        ],
        out_specs=[pl.BlockSpec((rows, D), lambda i: (i, 0)),
                   pl.BlockSpec((n_seq, FFN_CONV_W - 1, 2 * d_ff), lambda i: (i, 0, 0))],
        out_shape=[jax.ShapeDtypeStruct(x2d.shape, F32),
                   jax.ShapeDtypeStruct((DB, FFN_CONV_W - 1, 2 * d_ff), F32)],
        scratch_shapes=[pltpu.VMEM((n_seq, 8 + t_new, 2 * d_ff), F32), pltpu.VMEM((rows, d_ff), BF16)],
        compiler_params=pltpu.CompilerParams(dimension_semantics=("arbitrary",),
                                             vmem_limit_bytes=VMEM_LIMIT),
        name="ffn_sample",
    )(x2d, st_in, gpre, wup_bf16, wdw, bdw, wdown_bf16, gpost)


def _mix_sample_kernel(x_ref, g_ref, w_ref, q_ref, k_ref, v_ref, u_ref):
    h = _rms(x_ref[...], g_ref[...]).astype(BF16)
    z = jnp.dot(h, w_ref[...], preferred_element_type=F32)
    q_ref[...] = z[:, 0:ATTN_DIM] * (HEAD_DIM ** -0.5)
    k_ref[...] = z[:, ATTN_DIM:2 * ATTN_DIM]
    v_ref[...] = z[:, 2 * ATTN_DIM:3 * ATTN_DIM]
    u_ref[...] = z[:, 3 * ATTN_DIM:3 * ATTN_DIM + 512] * jax.nn.sigmoid(z[:, 3 * ATTN_DIM + 512:])


def _mix_sample(x2d, g, w_bf16, rows=256):
    N, D = x2d.shape
    rows = min(rows, N)
    tile = lambda i: (i, 0)
    out = jax.ShapeDtypeStruct((N, ATTN_DIM), F32)
    return pl.pallas_call(
        _mix_sample_kernel,
        grid=(N // rows,),
        in_specs=[pl.BlockSpec((rows, D), tile), _const_spec((1, D)), _const_spec(w_bf16.shape)],
        out_specs=[pl.BlockSpec((rows, ATTN_DIM), tile)] * 4,
        out_shape=[out] * 4,
        compiler_params=pltpu.CompilerParams(dimension_semantics=("arbitrary",),
                                             vmem_limit_bytes=VMEM_LIMIT),
        name="mix_sample",
    )(x2d, g, w_bf16)


def _attn_sample_kernel(pt_ref, q_ref, kn_ref, vn_ref, ab_ref, e2_ref, g_ref, *rest,
                        n_kv_steps, past, t_new):
    pages = PAGES_PER_STEP
    k_pages = rest[:pages]
    v_pages = rest[pages:2 * pages]
    o_ref = rest[2 * pages]
    logit_buf, p_buf, sc_buf, acc_buf, stat_buf = rest[2 * pages + 1:]
    i = pl.program_id(1)
    n_col = N_HEADS * t_new
    page_size = k_pages[0].shape[2]
    per_block = MOBA_BLOCK // page_size
    blocks_per_step = pages // per_block

    q = q_ref[0]
    row_head = lax.broadcasted_iota(jnp.int32, (n_col, ATTN_DIM), 0) // t_new
    lane_head = lax.broadcasted_iota(jnp.int32, (n_col, ATTN_DIM), 1) // HEAD_DIM
    own = row_head == lane_head
    qbd = jnp.where(own, jnp.concatenate([q] * N_HEADS, axis=0), 0.0)

    @pl.when(i == 0)
    def _():
        sc_buf[...] = jnp.zeros_like(sc_buf)

    @pl.when(i < n_kv_steps)
    def _():
        qb = qbd.astype(BF16)
        sc = sc_buf[...]
        lane = lax.broadcasted_iota(jnp.int32, sc.shape, 1)
        for blk in range(blocks_per_step):
            block_sum = None
            for p in range(blk * per_block, (blk + 1) * per_block):
                lg = jnp.dot(qb, k_pages[p][0].astype(BF16), preferred_element_type=F32)
                logit_buf[i * pages + p] = lg
                block_sum = lg if block_sum is None else block_sum + lg
            score = jnp.sum(block_sum, axis=1, keepdims=True) * (1.0 / MOBA_BLOCK)
            sc = jnp.where(lane == i * blocks_per_step + blk, score, sc)
        sc_buf[...] = sc

    @pl.when(i == n_kv_steps - 1)
    def _():
        nb = n_kv_steps * blocks_per_step
        sc = sc_buf[...]
        ab = ab_ref[...]
        lane = lax.broadcasted_iota(jnp.int32, ab.shape, 1)
        coef = jnp.where(lane < nb, _top_k_bias(sc, lane < nb, 1), ab).astype(BF16)
        bias = jnp.dot(coef, e2_ref[...], preferred_element_type=F32)
        n_pages = n_kv_steps * pages
        kn = kn_ref[0].astype(BF16)
        l_own = lax.dot_general(qbd.astype(BF16), kn, (((1,), (1,)), ((), ())), preferred_element_type=F32)
        key_t = lax.broadcasted_iota(jnp.int32, (n_col, t_new), 1)
        qry_t = lax.broadcasted_iota(jnp.int32, (n_col, t_new), 0) % t_new
        slope = ab[:, nb:nb + 1]
        l_own = l_own + slope * (past + key_t).astype(F32)
        l_own = jnp.where(key_t <= qry_t, l_own, NEG)
        lane_max = None
        for pg in range(n_pages):
            lg = logit_buf[pg] + bias[:, pg * page_size:(pg + 1) * page_size]
            logit_buf[pg] = lg
            lane_max = lg if lane_max is None else jnp.maximum(lane_max, lg)
        m = jnp.maximum(jnp.max(l_own, axis=1, keepdims=True), jnp.max(lane_max, axis=1, keepdims=True))
        p_own = jnp.exp(l_own - m)
        lane_sum = jnp.zeros((n_col, page_size), F32)
        for pg in range(n_pages):
            pe = jnp.exp(logit_buf[pg] - m)
            lane_sum = lane_sum + pe
            p_buf[pg] = pe.astype(BF16)
        l = jnp.sum(p_own, axis=1, keepdims=True) + jnp.sum(lane_sum, axis=1, keepdims=True)
        stat_buf[...] = jnp.broadcast_to(l, stat_buf.shape)
        acc_buf[...] = jnp.dot(p_own.astype(BF16), vn_ref[0].astype(BF16), preferred_element_type=F32)

    @pl.when(i >= n_kv_steps)
    def _():
        acc = acc_buf[...]
        for p in range(pages):
            acc = acc + lax.dot_general(p_buf[(i - n_kv_steps) * pages + p], v_pages[p][0].astype(BF16),
                                        (((1,), (1,)), ((), ())), preferred_element_type=F32)
        acc_buf[...] = acc

    @pl.when(i == 2 * n_kv_steps - 1)
    def _():
        o = acc_buf[...] / stat_buf[:, 0:1]
        o = jnp.where(own, o, 0.0)
        on = o * lax.rsqrt(jnp.sum(o * o, axis=1, keepdims=True) * (1.0 / HEAD_DIM) + EPS)
        out = on[0:t_new]
        for hh in range(1, N_HEADS):
            out = out + on[hh * t_new:(hh + 1) * t_new]
        o_ref[0] = (out * g_ref[...]).astype(o_ref.dtype)


def _attn_sample(page_table, q, kn, vn, cache_kT, cache_vT, ab, e2, g_attn):
    DB, t_new, _ = q.shape
    n_pages = page_table.shape[1]
    page_size = cache_kT.shape[2]
    past = n_pages * page_size
    pages = PAGES_PER_STEP
    n_kv_steps = n_pages // pages
    n_col = N_HEADS * t_new

    def k_map(p):
        def index(b, i, pt):
            return (pt[b * n_pages + jnp.minimum(i, n_kv_steps - 1) * pages + p], 0, 0)
        return index

    def v_map(p):
        def index(b, i, pt):
            in_v = i >= n_kv_steps
            seq = jnp.where(in_v, b, jnp.maximum(b - 1, 0))
            step = jnp.where(in_v, i - n_kv_steps, jnp.where(b > 0, n_kv_steps - 1, 0))
            return (pt[seq * n_pages + step * pages + p], 0, 0)
        return index

    per_seq = lambda b, i, pt: (b, 0, 0)
    page_block = (1, ATTN_DIM, page_size)
    grid_spec = pltpu.PrefetchScalarGridSpec(
        num_scalar_prefetch=1,
        grid=(DB, 2 * n_kv_steps),
        in_specs=[
            pl.BlockSpec((1, t_new, ATTN_DIM), per_seq),
            pl.BlockSpec((1, t_new, ATTN_DIM), per_seq),
            pl.BlockSpec((1, t_new, ATTN_DIM), per_seq),
            _const_spec(ab.shape), _const_spec(e2.shape), _const_spec((1, ATTN_DIM)),
        ] + [pl.BlockSpec(page_block, k_map(p)) for p in range(pages)]
          + [pl.BlockSpec(page_block, v_map(p)) for p in range(pages)],
        out_specs=pl.BlockSpec((1, t_new, ATTN_DIM), per_seq),
        scratch_shapes=[
            pltpu.VMEM((n_pages, n_col, page_size), F32),
            pltpu.VMEM((n_pages, n_col, page_size), BF16),
            pltpu.VMEM((n_col, LANES), F32),
            pltpu.VMEM((n_col, ATTN_DIM), F32),
            pltpu.VMEM((n_col, LANES), F32),
        ],
    )
    return pl.pallas_call(
        functools.partial(_attn_sample_kernel, n_kv_steps=n_kv_steps, past=past, t_new=t_new),
        grid_spec=grid_spec,
        out_shape=jax.ShapeDtypeStruct((DB, t_new, ATTN_DIM), BF16),
        compiler_params=pltpu.CompilerParams(dimension_semantics=("arbitrary", "arbitrary"),
                                             vmem_limit_bytes=VMEM_LIMIT),
        name="attn_sample",
    )(page_table.reshape(-1), q, kn, vn, ab, e2, g_attn, *([cache_kT] * pages), *([cache_vT] * pages))


def _alibi_slopes():
    return jnp.exp2(-8.0 * jnp.arange(1, N_HEADS + 1, dtype=F32) / N_HEADS)


def _position_tables(n_keys, nbp, t_new):
    slopes = _alibi_slopes()
    pos = jnp.arange(n_keys)
    blk, off = pos // MOBA_BLOCK, pos % MOBA_BLOCK
    lane = jnp.arange(LANES)
    kb = jnp.where(lane[None, :] == blk[:, None], 1.0, 0.0)
    for first in (nbp, nbp + 2):
        kb = jnp.where(lane[None, :] == first, off[:, None].astype(F32), kb)
        kb = jnp.where(lane[None, :] == first + 1, blk[:, None].astype(F32), kb)
    hi = (slopes * LOG2E).astype(BF16).astype(F32)
    lo = slopes * LOG2E - hi
    row = jnp.arange(LANES - nbp)[None, :]
    cq = (jnp.where(row == 0, hi[:, None], 0.0) + jnp.where(row == 1, hi[:, None] * MOBA_BLOCK, 0.0)
          + jnp.where(row == 2, lo[:, None], 0.0) + jnp.where(row == 3, lo[:, None] * MOBA_BLOCK, 0.0))
    cq = jnp.broadcast_to(cq[:, :, None], (N_HEADS, LANES - nbp, MOBA_BLOCK))
    ab = jnp.zeros((N_HEADS * t_new, LANES), F32)
    ab = ab.at[:, nbp].set(jnp.repeat(slopes, t_new)).at[:, nbp + 1].set(jnp.repeat(slopes, t_new) * MOBA_BLOCK)
    return kb.astype(BF16), cq.astype(BF16), ab


def kernel(x_prompt, x_sample, cache_k, cache_v, page_table, state_conv, state_ffn_conv, g_pre_mix, w_in,
           w_conv_dw, b_conv_dw, g_conv_ln, b_conv_ln, g_attn_out, g_conv_out, w_out, g_post_mix, g_pre_ffn,
           w_up, w_ffn_dw, b_ffn_dw, w_down, g_post_ffn):
    depth = w_in.shape[0]
    B, S, D = x_prompt.shape
    DB, T, _ = x_sample.shape
    n_pool, page_size = cache_k.shape[1], cache_k.shape[2]
    n_pages = page_table.shape[1]
    past = n_pages * page_size
    assert S % MOBA_BLOCK == 0 and S >= CONV_W - 1
    assert past % MOBA_BLOCK == 0 and past // MOBA_BLOCK >= MOBA_TOPK, "the cached past must be whole key blocks"
    assert n_pages % PAGES_PER_STEP == 0 and (PAGES_PER_STEP * page_size) % MOBA_BLOCK == 0
    assert T <= MOBA_BLOCK
    nb_p = S // MOBA_BLOCK
    nbp_p = _round_up(nb_p, BF16_SUBLANES)
    nb_s = past // MOBA_BLOCK
    assert nbp_p + 4 <= LANES and nb_s + 4 <= LANES

    kb_p, cq_p, _ = _position_tables(S, nbp_p, T)
    kb_s, _, ab_s = _position_tables(past, nb_s, T)
    e2_s = kb_s.T
    group = jnp.arange(512) // (512 // CONV_GROUPS)
    gmat = jnp.where(group[:, None] == group[None, :], 1.0 / (512 // CONV_GROUPS), 0.0).astype(BF16)

    xp, xs = x_prompt, x_sample.reshape(DB * T, D)
    row = lambda a: a.reshape(1, -1)
    outs = {k: [] for k in ("kp", "vp", "cp", "fp", "ks", "vs", "cs", "fs")}
    for l in range(depth):
        w_in_b, w_out_b = w_in[l].astype(BF16), w_out[l].astype(BF16)
        w_up_b, w_down_b = w_up[l].astype(BF16), w_down[l].astype(BF16)
        wdw = jnp.pad(w_conv_dw[l], ((0, HALO - CONV_W), (0, 0)))
        wfd = jnp.pad(w_ffn_dw[l], ((0, 8 - FFN_CONV_W), (0, 0)))
        conv_args = (wdw, row(b_conv_dw[l]), row(g_conv_ln[l]), row(b_conv_ln[l]), row(g_conv_out[l]), gmat,
                     w_out_b, row(g_post_mix[l]))
        ffn_args = (row(g_pre_ffn[l]), w_up_b, wfd, row(b_ffn_dw[l]), w_down_b, row(g_post_ffn[l]))

        qT, selT, kbf, vT, kp, vp, u = _mix_prompt(xp, row(g_pre_mix[l]), w_in_b, nbp_p)
        attn = _attn_prompt(qT, selT, cq_p, kbf, vT, kb_p, g_attn_out[l].reshape(-1, 1))
        xp = _merge_prompt(xp, attn, u, *conv_args)
        xp, fst_p = _ffn_prompt(xp, *ffn_args)
        outs["kp"].append(kp.reshape(B, S, N_HEADS, HEAD_DIM))
        outs["vp"].append(vp.reshape(B, S, N_HEADS, HEAD_DIM))
        outs["cp"].append(u[:, S - (CONV_W - 1):])
        outs["fp"].append(fst_p)

        q_s, k_s, v_s, u_s = _mix_sample(xs, row(g_pre_mix[l]), w_in_b)
        as3 = lambda a: a.reshape(DB, T, ATTN_DIM)
        pages_t = lambda c: jnp.transpose(c, (0, 2, 3, 1)).reshape(n_pool, ATTN_DIM, page_size)
        attn_s = _attn_sample(page_table, as3(q_s), as3(k_s), as3(v_s), pages_t(cache_k[l]), pages_t(cache_v[l]),
                              ab_s, e2_s, row(g_attn_out[l]))
        ctx = jnp.concatenate([state_conv[l], u_s.reshape(DB, T, 512)], axis=1)
        xs = _merge_sample(xs, attn_s.reshape(DB * T, ATTN_DIM), ctx, *conv_args)
        xs, fst_s = _ffn_sample(xs, state_ffn_conv[l], *ffn_args)
        outs["ks"].append(k_s.reshape(DB, T, N_HEADS, HEAD_DIM))
        outs["vs"].append(v_s.reshape(DB, T, N_HEADS, HEAD_DIM))
        outs["cs"].append(ctx[:, T:])
        outs["fs"].append(fst_s)

    st = lambda name: jnp.stack(outs[name])
    return (xp, xs.reshape(DB, T, D), st("kp"), st("vp"), st("cp"), st("fp"),
            st("ks"), st("vs"), st("cs"), st("fs"))
```

```python
import functools

import jax
import jax.numpy as jnp
from jax import lax
from jax.experimental import pallas as pl
from jax.experimental.pallas import tpu as pltpu

N_HEADS = 8
HEAD_DIM = 64
ATTN_DIM = N_HEADS * HEAD_DIM
CONV_GROUPS = 8
CONV_W = 31
MOBA_BLOCK = 256
MOBA_TOPK = 3
FFN_CONV_W = 3
EPS = 1e-6
NEG = -1e30
LOG2E = 1.4426950408889634

LANES = 128
SUBLANES = 8
BF16_SUBLANES = 16
HALO = 32
PAGES_PER_STEP = 16
DMA_SLOTS = 4
PREFETCH_CHUNKS = DMA_SLOTS - 1
VMEM_LIMIT = 56 * 1024 * 1024

F32 = jnp.float32
BF16 = jnp.bfloat16
HIGHEST = lax.Precision.HIGHEST


def _round_up(n, m):
    return -(-n // m) * m


def _const_spec(shape):
    zeros = (0,) * len(shape)
    return pl.BlockSpec(shape, lambda *_: zeros)


def _rms(x, g):
    return x * lax.rsqrt(jnp.mean(x * x, axis=-1, keepdims=True) + EPS) * g


def _split_dot(x, w_bf16):
    hi = x.astype(BF16)
    lo = (x - hi.astype(F32)).astype(BF16)
    return (jnp.dot(hi, w_bf16, preferred_element_type=F32)
            + jnp.dot(lo, w_bf16, preferred_element_type=F32))


def _top_k_bias(sc, cand, axis):
    n = sc.shape[axis]
    idx = lax.broadcasted_iota(jnp.int32, sc.shape, axis)
    s = jnp.where(cand, sc, -jnp.inf)
    bias = jnp.full(sc.shape, NEG, F32)
    for _ in range(MOBA_TOPK):
        m = jnp.max(s, axis=axis, keepdims=True)
        first = jnp.min(jnp.where(s == m, idx, n), axis=axis, keepdims=True)
        pick = idx == first
        bias = jnp.where(pick, 0.0, bias)
        s = jnp.where(pick, -jnp.inf, s)
    return jnp.where(cand, bias, NEG)


def _mix_prompt_kernel(x_ref, g_ref, w_ref, qT_ref, selT_ref, kbf_ref, vT_ref, k_ref, v_ref, u_ref,
                       km_ref):
    j = pl.program_id(1)
    nbp = km_ref.shape[0]

    @pl.when(j == 0)
    def _():
        km_ref[...] = jnp.zeros_like(km_ref)

    h = _rms(x_ref[0], g_ref[...]).astype(BF16)
    z = jnp.dot(h, w_ref[...], preferred_element_type=F32)
    q = z[:, 0:ATTN_DIM] * (HEAD_DIM ** -0.5 * LOG2E)
    k = z[:, ATTN_DIM:2 * ATTN_DIM]
    v = z[:, 2 * ATTN_DIM:3 * ATTN_DIM]
    a = z[:, 3 * ATTN_DIM:3 * ATTN_DIM + 512]
    gg = z[:, 3 * ATTN_DIM + 512:]
    k_ref[0] = k
    v_ref[0] = v
    kbf_ref[0] = k.astype(BF16)
    u_ref[0] = a * jax.nn.sigmoid(gg)
    qT = q.T
    qT_ref[0] = qT.astype(BF16)
    vT_ref[0, 0] = v.T.astype(BF16)

    km = km_ref[...]
    blk = lax.broadcasted_iota(jnp.int32, (nbp, MOBA_BLOCK), 0)
    lane_head = lax.broadcasted_iota(jnp.int32, (nbp, LANES), 1) // HEAD_DIM
    for hh in range(N_HEADS):
        pr = hh // 2
        kmp = jnp.where(lane_head == hh % 2, km[:, pr * LANES:(pr + 1) * LANES], 0.0)
        sc = jnp.dot(kmp, qT[pr * LANES:(pr + 1) * LANES, :], preferred_element_type=F32,
                     precision=HIGHEST)
        selT_ref[0, hh] = _top_k_bias(sc, blk < j, 0).astype(BF16)

    row = lax.broadcasted_iota(jnp.int32, km.shape, 0)
    km_ref[...] = jnp.where(row == j, jnp.mean(k, axis=0, keepdims=True), km)


def _mix_prompt(x, g, w_bf16, nbp):
    B, S, D = x.shape
    nb = S // MOBA_BLOCK
    n_out = w_bf16.shape[1]
    tile = lambda b, j: (b, j, 0)
    return pl.pallas_call(
        _mix_prompt_kernel,
        grid=(B, nb),
        in_specs=[pl.BlockSpec((1, MOBA_BLOCK, D), tile), _const_spec((1, D)), _const_spec((D, n_out))],
        out_specs=[
            pl.BlockSpec((1, ATTN_DIM, MOBA_BLOCK), lambda b, j: (b, 0, j)),
            pl.BlockSpec((1, N_HEADS, nbp, MOBA_BLOCK), lambda b, j: (b, 0, 0, j)),
            pl.BlockSpec((1, MOBA_BLOCK, ATTN_DIM), tile),
            pl.BlockSpec((1, 1, ATTN_DIM, MOBA_BLOCK), lambda b, j: (b, j, 0, 0)),
            pl.BlockSpec((1, MOBA_BLOCK, ATTN_DIM), tile),
            pl.BlockSpec((1, MOBA_BLOCK, ATTN_DIM), tile),
            pl.BlockSpec((1, MOBA_BLOCK, 512), tile),
        ],
        out_shape=[
            jax.ShapeDtypeStruct((B, ATTN_DIM, S), BF16),
            jax.ShapeDtypeStruct((B, N_HEADS, nbp, S), BF16),
            jax.ShapeDtypeStruct((B, S, ATTN_DIM), BF16),
            jax.ShapeDtypeStruct((B, nb, ATTN_DIM, MOBA_BLOCK), BF16),
            jax.ShapeDtypeStruct((B, S, ATTN_DIM), F32),
            jax.ShapeDtypeStruct((B, S, ATTN_DIM), F32),
            jax.ShapeDtypeStruct((B, S, 512), F32),
        ],
        scratch_shapes=[pltpu.VMEM((nbp, ATTN_DIM), F32)],
        compiler_params=pltpu.CompilerParams(dimension_semantics=("arbitrary", "arbitrary"),
                                             vmem_limit_bytes=VMEM_LIMIT),
        name="mix_prompt",
    )(x, g, w_bf16)


def _attn_prompt_kernel(qT_ref, selT_ref, cq_ref, k_ref, vT_ref, kb_ref, g_ref, o_ref, s_buf, p_buf, *, unroll):
    j = pl.program_id(2)
    nb = vT_ref.shape[1]
    nbp = selT_ref.shape[2]
    qT = qT_ref[0]
    sub_head = lax.broadcasted_iota(jnp.int32, qT.shape, 0) // HEAD_DIM
    no_sel = jnp.zeros((nbp, MOBA_BLOCK), BF16)
    qq, qq_own = [], []
    for hh in range(2):
        qm = jnp.where(sub_head == hh, qT, jnp.zeros_like(qT))
        qq.append(jnp.concatenate([qm, selT_ref[0, hh], cq_ref[hh]], axis=0))
        qq_own.append(jnp.concatenate([qm, no_sel, cq_ref[hh]], axis=0))

    def stage_a(b, slot, rhs, causal):
        off = pl.multiple_of(b * MOBA_BLOCK, MOBA_BLOCK)
        kk = jnp.concatenate([k_ref[0, pl.ds(off, MOBA_BLOCK), :], kb_ref[pl.ds(off, MOBA_BLOCK), :]], axis=1)
        col_max = []
        for hh in range(2):
            s = jnp.dot(kk, rhs[hh], preferred_element_type=F32)
            if causal:
                key_i = lax.broadcasted_iota(jnp.int32, s.shape, 0)
                qry_i = lax.broadcasted_iota(jnp.int32, s.shape, 1)
                s = jnp.where(key_i <= qry_i, s, NEG)
            s_buf[slot, hh] = s
            col_max.append(jnp.max(s, axis=0, keepdims=True))
        return col_max

    def stage_b(slot, col_max, m):
        m_out, alpha_out = [], []
        for hh in range(2):
            m_new = jnp.maximum(m[hh], col_max[hh])
            p_buf[slot, hh] = jnp.exp2(s_buf[slot, hh] - m_new).astype(BF16)
            m_out.append(m_new)
            alpha_out.append(jnp.exp2(m[hh] - m_new))
        return m_out, alpha_out

    ones = jnp.ones((BF16_SUBLANES, MOBA_BLOCK), BF16)

    def stage_c(b, slot, alpha, acc):
        vt = vT_ref[0, b]
        return [alpha[hh] * acc[hh]
                + jnp.dot(jnp.concatenate([vt[hh * HEAD_DIM:(hh + 1) * HEAD_DIM, :], ones], axis=0),
                          p_buf[slot, hh], preferred_element_type=F32) for hh in range(2)]

    minus_inf = [jnp.full((1, MOBA_BLOCK), -jnp.inf, F32)] * 2
    m, alpha_c = stage_b(1, stage_a(j, 1, qq_own, True), minus_inf)
    cm0 = stage_a(0, 0, qq, False)
    acc = [jnp.zeros((HEAD_DIM + BF16_SUBLANES, MOBA_BLOCK), F32)] * 2

    def trip(t, carry):
        m, acc, col_max, alpha = carry
        for u in range(unroll):
            b, slot = unroll * t + u, u % 2
            nxt = b + 1 if u < unroll - 1 else jnp.minimum(b + 1, nb - 1)
            prev = b - 1 if u > 0 else jnp.where(t == 0, j, b - 1)
            next_max = stage_a(nxt, 1 - slot, qq, False)
            m, alpha_b = stage_b(slot, col_max, m)
            acc = stage_c(prev, 1 - slot, alpha, acc)
            col_max, alpha = next_max, alpha_b
        return m, acc, col_max, alpha

    n_trips = (j + unroll - 1) // unroll
    m, acc, cm0, alpha_c = lax.fori_loop(0, n_trips, trip, (m, acc, cm0, alpha_c))
    acc = stage_c(jnp.where(n_trips == 0, j, unroll * n_trips - 1), 1, alpha_c, acc)

    outs = []
    for hh in range(2):
        o = acc[hh][0:HEAD_DIM] / acc[hh][HEAD_DIM:HEAD_DIM + 1]
        outs.append(o * lax.rsqrt(jnp.mean(o * o, axis=0, keepdims=True) + EPS))
    on = jnp.concatenate(outs, axis=0) * g_ref[...]
    o_ref[0] = on.T.astype(o_ref.dtype)


def _attn_prompt(qT, selT, cq, kbf, vT, kb, g_col):
    B, _, S = qT.shape
    nb = S // MOBA_BLOCK
    nbp = selT.shape[2]
    unroll = 4 if nb % 4 == 0 else 2
    return pl.pallas_call(
        functools.partial(_attn_prompt_kernel, unroll=unroll),
        grid=(B, N_HEADS // 2, nb),
        in_specs=[
            pl.BlockSpec((1, LANES, MOBA_BLOCK), lambda b, p, j: (b, p, j)),
            pl.BlockSpec((1, 2, nbp, MOBA_BLOCK), lambda b, p, j: (b, p, 0, j)),
            pl.BlockSpec((2, LANES - nbp, MOBA_BLOCK), lambda b, p, j: (p, 0, 0)),
            pl.BlockSpec((1, S, LANES), lambda b, p, j: (b, 0, p)),
            pl.BlockSpec((1, nb, LANES, MOBA_BLOCK), lambda b, p, j: (b, 0, p, 0)),
            _const_spec((S, LANES)),
            pl.BlockSpec((LANES, 1), lambda b, p, j: (p, 0)),
        ],
        out_specs=pl.BlockSpec((1, MOBA_BLOCK, LANES), lambda b, p, j: (b, j, p)),
        out_shape=jax.ShapeDtypeStruct((B, S, ATTN_DIM), BF16),
        scratch_shapes=[pltpu.VMEM((2, 2, MOBA_BLOCK, MOBA_BLOCK), F32),
                        pltpu.VMEM((2, 2, MOBA_BLOCK, MOBA_BLOCK), BF16)],
        compiler_params=pltpu.CompilerParams(dimension_semantics=("arbitrary",) * 3,
                                             vmem_limit_bytes=VMEM_LIMIT),
        name="attn_prompt",
    )(qT, selT, cq, kbf, vT, kb, g_col)


def _conv_branch_out(c, lng, lnb, gco, gmat):
    mu = jnp.mean(c, axis=-1, keepdims=True)
    d = c - mu
    y = d * lax.rsqrt(jnp.mean(d * d, axis=-1, keepdims=True) + EPS) * lng + lnb
    y = y * jax.nn.sigmoid(y)
    gms = _split_dot(y * y, gmat)
    return (y * lax.rsqrt(gms + EPS) * gco).astype(BF16)


def _merge_tail(x, attn_n, conv_n, wout_ref, gpost):
    o = (jnp.dot(attn_n, wout_ref[0:ATTN_DIM, :], preferred_element_type=F32)
         + jnp.dot(conv_n, wout_ref[ATTN_DIM:, :], preferred_element_type=F32))
    return x + _rms(o, gpost)


def _merge_prompt_kernel(x_ref, attn_ref, u_ref, uh_ref, wdw_ref, bdw_ref, lng_ref, lnb_ref, gco_ref,
                         gmat_ref, wout_ref, gpost_ref, o_ref, ubuf, sbuf, cbuf, *, rows, chunk):
    i = pl.program_id(1)
    ubuf[0:HALO, :] = jnp.where(i > 0, uh_ref[0], 0.0)
    ubuf[HALO:, :] = u_ref[0]
    span = sbuf.shape[1]
    for shift in range(1, SUBLANES):
        sbuf[shift - 1] = ubuf[shift:shift + span, :]
    first = HALO - (CONV_W - 1)
    tiles = chunk // SUBLANES
    for r0 in range(0, rows, chunk):
        acc = jnp.broadcast_to(bdw_ref[...], (tiles, SUBLANES, 512))
        for w in range(CONV_W):
            shift, base = (first + w) % SUBLANES, (first + w) // SUBLANES * SUBLANES + r0
            window = ubuf[base:base + chunk, :] if shift == 0 else sbuf[shift - 1, base:base + chunk, :]
            acc = acc + wdw_ref[w] * window.reshape(tiles, SUBLANES, 512)
        cbuf[r0:r0 + chunk, :] = acc.reshape(chunk, 512)
    conv_n = _conv_branch_out(cbuf[...], lng_ref[...], lnb_ref[...], gco_ref[...], gmat_ref[...])
    o_ref[0] = _merge_tail(x_ref[0], attn_ref[0], conv_n, wout_ref, gpost_ref[...])


def _merge_prompt(x, attn_n, u, wdw, bdw, lng, lnb, gco, gmat, wout_bf16, gpost, rows=256, chunk=32):
    B, S, D = x.shape
    tile = lambda b, i: (b, i, 0)
    halo_blocks = rows // HALO
    return pl.pallas_call(
        functools.partial(_merge_prompt_kernel, rows=rows, chunk=chunk),
        grid=(B, S // rows),
        in_specs=[
            pl.BlockSpec((1, rows, D), tile),
            pl.BlockSpec((1, rows, ATTN_DIM), tile),
            pl.BlockSpec((1, rows, 512), tile),
            pl.BlockSpec((1, HALO, 512), lambda b, i: (b, jnp.maximum(i * halo_blocks - 1, 0), 0)),
            _const_spec(wdw.shape), _const_spec((1, 512)), _const_spec((1, 512)), _const_spec((1, 512)),
            _const_spec((1, 512)), _const_spec((512, 512)), _const_spec(wout_bf16.shape), _const_spec((1, D)),
        ],
        out_specs=pl.BlockSpec((1, rows, D), tile),
        out_shape=jax.ShapeDtypeStruct((B, S, D), F32),
        scratch_shapes=[pltpu.VMEM((HALO + rows, 512), F32),
                        pltpu.VMEM((SUBLANES - 1, HALO + rows - SUBLANES, 512), F32),
                        pltpu.VMEM((rows, 512), F32)],
        compiler_params=pltpu.CompilerParams(dimension_semantics=("arbitrary", "arbitrary"),
                                             vmem_limit_bytes=VMEM_LIMIT),
        name="merge_prompt",
    )(x, attn_n, u, u, wdw, bdw, lng, lnb, gco, gmat, wout_bf16, gpost)


def _merge_sample_kernel(x_ref, attn_ref, ctx_ref, wdw_ref, bdw_ref, lng_ref, lnb_ref, gco_ref,
                         gmat_ref, wout_ref, gpost_ref, o_ref, cbuf, *, n_seq, t_new):
    def one_seq(s, _):
        acc = jnp.broadcast_to(bdw_ref[...], (t_new, 512))
        for w in range(CONV_W):
            acc = acc + wdw_ref[w:w + 1, :] * ctx_ref[s, w:w + t_new, :]
        cbuf[pl.ds(pl.multiple_of(s * t_new, t_new), t_new), :] = acc
        return 0

    lax.fori_loop(0, n_seq, one_seq, 0)
    conv_n = _conv_branch_out(cbuf[...], lng_ref[...], lnb_ref[...], gco_ref[...], gmat_ref[...])
    o_ref[...] = _merge_tail(x_ref[...], attn_ref[...], conv_n, wout_ref, gpost_ref[...])


def _merge_sample(x2d, attn_n, ctx, wdw, bdw, lng, lnb, gco, gmat, wout_bf16, gpost, n_seq=32):
    DB, ctx_len, _ = ctx.shape
    n_seq = min(n_seq, DB)
    t_new = ctx_len - (CONV_W - 1)
    D = x2d.shape[1]
    rows = n_seq * t_new
    tile = lambda i: (i, 0)
    return pl.pallas_call(
        functools.partial(_merge_sample_kernel, n_seq=n_seq, t_new=t_new),
        grid=(DB // n_seq,),
        in_specs=[
            pl.BlockSpec((rows, D), tile),
            pl.BlockSpec((rows, ATTN_DIM), tile),
            pl.BlockSpec((n_seq, ctx_len, 512), lambda i: (i, 0, 0)),
            _const_spec(wdw.shape), _const_spec((1, 512)), _const_spec((1, 512)), _const_spec((1, 512)),
            _const_spec((1, 512)), _const_spec((512, 512)), _const_spec(wout_bf16.shape), _const_spec((1, D)),
        ],
        out_specs=pl.BlockSpec((rows, D), tile),
        out_shape=jax.ShapeDtypeStruct(x2d.shape, F32),
        scratch_shapes=[pltpu.VMEM((rows, 512), F32)],
        compiler_params=pltpu.CompilerParams(dimension_semantics=("arbitrary",),
                                             vmem_limit_bytes=VMEM_LIMIT),
        name="merge_sample",
    )(x2d, attn_n, ctx, wdw, bdw, lng, lnb, gco, gmat, wout_bf16, gpost)


def _gated(c_gate, c_val):
    return (jax.nn.gelu(c_gate, approximate=True) * c_val).astype(BF16)


def _ffn_prompt_kernel(x_ref, gpre_ref, wup_ref, wdw_ref, bdw_ref, wdown_ref, gpost_ref, o_ref, st_ref,
                       upbuf, fbuf, *, rows, d_ff, chunk):
    i = pl.program_id(1)

    @pl.when(i == 0)
    def _():
        upbuf[0:8, :] = jnp.zeros((8, 2 * d_ff), F32)

    @pl.when(i > 0)
    def _():
        upbuf[0:8, :] = upbuf[rows:rows + 8, :]

    x = x_ref[0]
    h = _rms(x, gpre_ref[...]).astype(BF16)
    upbuf[8:, :] = jnp.dot(h, wup_ref[...], preferred_element_type=F32)
    st_ref[0] = upbuf[rows + 8 - (FFN_CONV_W - 1):rows + 8, :]
    first = 8 - (FFN_CONV_W - 1)
    for c0 in range(0, d_ff, chunk):
        halves = []
        for base in (c0, d_ff + c0):
            acc = jnp.broadcast_to(bdw_ref[:, base:base + chunk], (rows, chunk))
            for w in range(FFN_CONV_W):
                acc = acc + wdw_ref[w:w + 1, base:base + chunk] * upbuf[first + w:first + w + rows, base:base + chunk]
            halves.append(acc)
        fbuf[:, c0:c0 + chunk] = _gated(*halves)
    f = jnp.dot(fbuf[...], wdown_ref[...], preferred_element_type=F32)
    o_ref[0] = x + _rms(f, gpost_ref[...])


def _ffn_prompt(x, gpre, wup_bf16, wdw, bdw, wdown_bf16, gpost, rows=256, chunk=256):
    B, S, D = x.shape
    d_ff = wdown_bf16.shape[0]
    tile = lambda b, i: (b, i, 0)
    return pl.pallas_call(
        functools.partial(_ffn_prompt_kernel, rows=rows, d_ff=d_ff, chunk=chunk),
        grid=(B, S // rows),
        in_specs=[
            pl.BlockSpec((1, rows, D), tile), _const_spec((1, D)), _const_spec(wup_bf16.shape),
            _const_spec(wdw.shape), _const_spec((1, 2 * d_ff)), _const_spec(wdown_bf16.shape),
            _const_spec((1, D)),
        ],
        out_specs=[pl.BlockSpec((1, rows, D), tile),
                   pl.BlockSpec((1, FFN_CONV_W - 1, 2 * d_ff), lambda b, i: (b, 0, 0))],
        out_shape=[jax.ShapeDtypeStruct((B, S, D), F32),
                   jax.ShapeDtypeStruct((B, FFN_CONV_W - 1, 2 * d_ff), F32)],
        scratch_shapes=[pltpu.VMEM((8 + rows, 2 * d_ff), F32), pltpu.VMEM((rows, d_ff), BF16)],
        compiler_params=pltpu.CompilerParams(dimension_semantics=("arbitrary", "arbitrary"),
                                             vmem_limit_bytes=VMEM_LIMIT),
        name="ffn_prompt",
    )(x, gpre, wup_bf16, wdw, bdw, wdown_bf16, gpost)


def _ffn_sample_kernel(x_ref, st_in_ref, gpre_ref, wup_ref, wdw_ref, bdw_ref, wdown_ref, gpost_ref,
                       o_ref, st_ref, upbuf, fbuf, *, n_seq, t_new, d_ff, chunk):
    x = x_ref[...]
    h = _rms(x, gpre_ref[...]).astype(BF16)
    up = jnp.dot(h, wup_ref[...], preferred_element_type=F32)
    first = 8 - (FFN_CONV_W - 1)
    upbuf[:, first:8, :] = st_in_ref[...]
    upbuf[:, 8:, :] = up.reshape(n_seq, t_new, 2 * d_ff)
    st_ref[...] = upbuf[:, 8 + t_new - (FFN_CONV_W - 1):8 + t_new, :]
    for c0 in range(0, d_ff, chunk):
        halves = []
        for base in (c0, d_ff + c0):
            acc = jnp.broadcast_to(bdw_ref[:, base:base + chunk].reshape(1, 1, chunk), (n_seq, t_new, chunk))
            for w in range(FFN_CONV_W):
                acc = acc + (wdw_ref[w:w + 1, base:base + chunk].reshape(1, 1, chunk)
                             * upbuf[:, first + w:first + w + t_new, base:base + chunk])
            halves.append(acc.reshape(n_seq * t_new, chunk))
        fbuf[:, c0:c0 + chunk] = _gated(*halves)
    f = jnp.dot(fbuf[...], wdown_ref[...], preferred_element_type=F32)
    o_ref[...] = x + _rms(f, gpost_ref[...])


def _ffn_sample(x2d, st_in, gpre, wup_bf16, wdw, bdw, wdown_bf16, gpost, n_seq=16, chunk=256):
    DB = st_in.shape[0]
    n_seq = min(n_seq, DB)
    D = x2d.shape[1]
    t_new = x2d.shape[0] // DB
    d_ff = wdown_bf16.shape[0]
    rows = n_seq * t_new
    return pl.pallas_call(
        functools.partial(_ffn_sample_kernel, n_seq=n_seq, t_new=t_new, d_ff=d_ff, chunk=chunk),
        grid=(DB // n_seq,),
        in_specs=[
            pl.BlockSpec((rows, D), lambda i: (i, 0)),
            pl.BlockSpec((n_seq, FFN_CONV_W - 1, 2 * d_ff), lambda i: (i, 0, 0)),
            _const_spec((1, D)), _const_spec(wup_bf16.shape), _const_spec(wdw.shape),
            _const_spec((1, 2 * d_ff)), _const_spec(wdown_bf16.shape), _const_spec((1, D)),
        ],
        out_specs=[pl.BlockSpec((rows, D), lambda i: (i, 0)),
                   pl.BlockSpec((n_seq, FFN_CONV_W - 1, 2 * d_ff), lambda i: (i, 0, 0))],
        out_shape=[jax.ShapeDtypeStruct(x2d.shape, F32),
                   jax.ShapeDtypeStruct((DB, FFN_CONV_W - 1, 2 * d_ff), F32)],
        scratch_shapes=[pltpu.VMEM((n_seq, 8 + t_new, 2 * d_ff), F32), pltpu.VMEM((rows, d_ff), BF16)],
        compiler_params=pltpu.CompilerParams(dimension_semantics=("arbitrary",),
                                             vmem_limit_bytes=VMEM_LIMIT),
        name="ffn_sample",
    )(x2d, st_in, gpre, wup_bf16, wdw, bdw, wdown_bf16, gpost)


def _mix_sample_kernel(x_ref, g_ref, w_ref, q_ref, k_ref, v_ref, u_ref):
    h = _rms(x_ref[...], g_ref[...]).astype(BF16)
    z = jnp.dot(h, w_ref[...], preferred_element_type=F32)
    q_ref[...] = z[:, 0:ATTN_DIM] * (HEAD_DIM ** -0.5)
    k_ref[...] = z[:, ATTN_DIM:2 * ATTN_DIM]
    v_ref[...] = z[:, 2 * ATTN_DIM:3 * ATTN_DIM]
    u_ref[...] = z[:, 3 * ATTN_DIM:3 * ATTN_DIM + 512] * jax.nn.sigmoid(z[:, 3 * ATTN_DIM + 512:])


def _mix_sample(x2d, g, w_bf16, rows=256):
    N, D = x2d.shape
    rows = min(rows, N)
    tile = lambda i: (i, 0)
    out = jax.ShapeDtypeStruct((N, ATTN_DIM), F32)
    return pl.pallas_call(
        _mix_sample_kernel,
        grid=(N // rows,),
        in_specs=[pl.BlockSpec((rows, D), tile), _const_spec((1, D)), _const_spec(w_bf16.shape)],
        out_specs=[pl.BlockSpec((rows, ATTN_DIM), tile)] * 4,
        out_shape=[out] * 4,
        compiler_params=pltpu.CompilerParams(dimension_semantics=("arbitrary",),
                                             vmem_limit_bytes=VMEM_LIMIT),
        name="mix_sample",
    )(x2d, g, w_bf16)


def _attn_sample_kernel(pt_ref, q_ref, kn_ref, vn_ref, ab_ref, e2_ref, g_ref, kT_hbm, vT_hbm, o_ref,
                        page_buf, sems, logit_buf, p_buf, *, n_pages, past, t_new):
    b = pl.program_id(0)
    n_seq = pl.num_programs(0)
    pages = PAGES_PER_STEP
    n_kv = n_pages // pages
    n_chunks = 2 * n_kv
    page_size = page_buf.shape[3]
    per_block = MOBA_BLOCK // page_size
    blocks_per_chunk = pages // per_block
    nb = n_pages // per_block
    n_col = N_HEADS * t_new

    def page_copy(seq, chunk, p):
        src = kT_hbm if chunk < n_kv else vT_hbm
        page = pt_ref[seq * n_pages + (chunk % n_kv) * pages + p]
        slot = chunk % DMA_SLOTS
        return pltpu.make_async_copy(src.at[page], page_buf.at[slot, p], sems.at[slot])

    def start_chunk(seq, chunk):
        for p in range(pages):
            page_copy(seq, chunk, p).start()

    def wait_chunk(seq, chunk):
        for p in range(pages):
            page_copy(seq, chunk, p).wait()

    def prefetch(chunk):
        if chunk < n_chunks:
            start_chunk(b, chunk)
        else:
            @pl.when(b + 1 < n_seq)
            def _():
                start_chunk(b + 1, chunk - n_chunks)

    @pl.when(b == 0)
    def _():
        for c in range(PREFETCH_CHUNKS):
            start_chunk(0, c)

    q = q_ref[0]
    row_head = lax.broadcasted_iota(jnp.int32, (n_col, ATTN_DIM), 0) // t_new
    lane_head = lax.broadcasted_iota(jnp.int32, (n_col, ATTN_DIM), 1) // HEAD_DIM
    own = row_head == lane_head
    qbd = jnp.where(own, jnp.concatenate([q] * N_HEADS, axis=0), 0.0)
    qb = qbd.astype(BF16)

    sc = jnp.zeros((n_col, LANES), F32)
    lane = lax.broadcasted_iota(jnp.int32, sc.shape, 1)
    for c in range(n_kv):
        prefetch(c + PREFETCH_CHUNKS)
        wait_chunk(b, c)
        for blk in range(blocks_per_chunk):
            block_sum = None
            for p in range(blk * per_block, (blk + 1) * per_block):
                lg = jnp.dot(qb, page_buf[c % DMA_SLOTS, p].astype(BF16), preferred_element_type=F32)
                logit_buf[c * pages + p] = lg
                block_sum = lg if block_sum is None else block_sum + lg
            score = jnp.sum(block_sum, axis=1, keepdims=True) * (1.0 / MOBA_BLOCK)
            sc = jnp.where(lane == c * blocks_per_chunk + blk, score, sc)

    ab = ab_ref[...]
    coef = jnp.where(lane < nb, _top_k_bias(sc, lane < nb, 1), ab).astype(BF16)
    bias = jnp.dot(coef, e2_ref[...], preferred_element_type=F32)
    l_own = lax.dot_general(qb, kn_ref[0].astype(BF16), (((1,), (1,)), ((), ())), preferred_element_type=F32)
    key_t = lax.broadcasted_iota(jnp.int32, (n_col, t_new), 1)
    qry_t = lax.broadcasted_iota(jnp.int32, (n_col, t_new), 0) % t_new
    l_own = l_own + ab[:, nb:nb + 1] * (past + key_t).astype(F32)
    l_own = jnp.where(key_t <= qry_t, l_own, NEG)
    lane_max = None
    for pg in range(n_pages):
        lg = logit_buf[pg] + bias[:, pg * page_size:(pg + 1) * page_size]
        logit_buf[pg] = lg
        lane_max = lg if lane_max is None else jnp.maximum(lane_max, lg)
    m = jnp.maximum(jnp.max(l_own, axis=1, keepdims=True), jnp.max(lane_max, axis=1, keepdims=True))
    p_own = jnp.exp(l_own - m)
    lane_sum = jnp.zeros((n_col, page_size), F32)
    for pg in range(n_pages):
        pe = jnp.exp(logit_buf[pg] - m)
        lane_sum = lane_sum + pe
        p_buf[pg] = pe.astype(BF16)
    denom = jnp.sum(p_own, axis=1, keepdims=True) + jnp.sum(lane_sum, axis=1, keepdims=True)

    acc = jnp.dot(p_own.astype(BF16), vn_ref[0].astype(BF16), preferred_element_type=F32)
    for c in range(n_kv, n_chunks):
        prefetch(c + PREFETCH_CHUNKS)
        wait_chunk(b, c)
        for p in range(pages):
            acc = acc + lax.dot_general(p_buf[(c - n_kv) * pages + p], page_buf[c % DMA_SLOTS, p].astype(BF16),
                                        (((1,), (1,)), ((), ())), preferred_element_type=F32)

    o = jnp.where(own, acc / denom, 0.0)
    on = o * lax.rsqrt(jnp.sum(o * o, axis=1, keepdims=True) * (1.0 / HEAD_DIM) + EPS)
    out = on[0:t_new]
    for hh in range(1, N_HEADS):
        out = out + on[hh * t_new:(hh + 1) * t_new]
    o_ref[0] = (out * g_ref[...]).astype(o_ref.dtype)


def _attn_sample(page_table, q, kn, vn, cache_kT, cache_vT, ab, e2, g_attn):
    DB, t_new, _ = q.shape
    n_pages = page_table.shape[1]
    page_size = cache_kT.shape[2]
    n_col = N_HEADS * t_new
    per_seq = lambda b, pt: (b, 0, 0)
    grid_spec = pltpu.PrefetchScalarGridSpec(
        num_scalar_prefetch=1,
        grid=(DB,),
        in_specs=[
            pl.BlockSpec((1, t_new, ATTN_DIM), per_seq),
            pl.BlockSpec((1, t_new, ATTN_DIM), per_seq),
            pl.BlockSpec((1, t_new, ATTN_DIM), per_seq),
            _const_spec(ab.shape), _const_spec(e2.shape), _const_spec((1, ATTN_DIM)),
            pl.BlockSpec(memory_space=pl.ANY), pl.BlockSpec(memory_space=pl.ANY),
        ],
        out_specs=pl.BlockSpec((1, t_new, ATTN_DIM), per_seq),
        scratch_shapes=[
            pltpu.VMEM((DMA_SLOTS, PAGES_PER_STEP, ATTN_DIM, page_size), F32),
            pltpu.SemaphoreType.DMA((DMA_SLOTS,)),
            pltpu.VMEM((n_pages, n_col, page_size), F32),
            pltpu.VMEM((n_pages, n_col, page_size), BF16),
        ],
    )
    return pl.pallas_call(
        functools.partial(_attn_sample_kernel, n_pages=n_pages, past=n_pages * page_size, t_new=t_new),
        grid_spec=grid_spec,
        out_shape=jax.ShapeDtypeStruct((DB, t_new, ATTN_DIM), BF16),
        compiler_params=pltpu.CompilerParams(dimension_semantics=("arbitrary",),
                                             vmem_limit_bytes=VMEM_LIMIT),
        name="attn_sample",
    )(page_table.reshape(-1), q, kn, vn, ab, e2, g_attn, cache_kT, cache_vT)


def _alibi_slopes():
    return jnp.exp2(-8.0 * jnp.arange(1, N_HEADS + 1, dtype=F32) / N_HEADS)


def _position_tables(n_keys, nbp, t_new):
    slopes = _alibi_slopes()
    pos = jnp.arange(n_keys)
    blk, off = pos // MOBA_BLOCK, pos % MOBA_BLOCK
    lane = jnp.arange(LANES)
    kb = jnp.where(lane[None, :] == blk[:, None], 1.0, 0.0)
    for first in (nbp, nbp + 2):
        kb = jnp.where(lane[None, :] == first, off[:, None].astype(F32), kb)
        kb = jnp.where(lane[None, :] == first + 1, blk[:, None].astype(F32), kb)
    hi = (slopes * LOG2E).astype(BF16).astype(F32)
    lo = slopes * LOG2E - hi
    row = jnp.arange(LANES - nbp)[None, :]
    cq = (jnp.where(row == 0, hi[:, None], 0.0) + jnp.where(row == 1, hi[:, None] * MOBA_BLOCK, 0.0)
          + jnp.where(row == 2, lo[:, None], 0.0) + jnp.where(row == 3, lo[:, None] * MOBA_BLOCK, 0.0))
    cq = jnp.broadcast_to(cq[:, :, None], (N_HEADS, LANES - nbp, MOBA_BLOCK))
    ab = jnp.zeros((N_HEADS * t_new, LANES), F32)
    ab = ab.at[:, nbp].set(jnp.repeat(slopes, t_new)).at[:, nbp + 1].set(jnp.repeat(slopes, t_new) * MOBA_BLOCK)
    return kb.astype(BF16), cq.astype(BF16), ab


def kernel(x_prompt, x_sample, cache_k, cache_v, page_table, state_conv, state_ffn_conv, g_pre_mix, w_in,
           w_conv_dw, b_conv_dw, g_conv_ln, b_conv_ln, g_attn_out, g_conv_out, w_out, g_post_mix, g_pre_ffn,
           w_up, w_ffn_dw, b_ffn_dw, w_down, g_post_ffn):
    depth = w_in.shape[0]
    B, S, D = x_prompt.shape
    DB, T, _ = x_sample.shape
    n_pool, page_size = cache_k.shape[1], cache_k.shape[2]
    n_pages = page_table.shape[1]
    past = n_pages * page_size
    assert S % MOBA_BLOCK == 0 and S >= CONV_W - 1
    assert past % MOBA_BLOCK == 0 and past // MOBA_BLOCK >= MOBA_TOPK, "the cached past must be whole key blocks"
    assert n_pages % PAGES_PER_STEP == 0 and (PAGES_PER_STEP * page_size) % MOBA_BLOCK == 0
    assert (2 * n_pages // PAGES_PER_STEP) % DMA_SLOTS == 0, "ring slots must line up across sequences"
    assert T <= MOBA_BLOCK
    nb_p = S // MOBA_BLOCK
    nbp_p = _round_up(nb_p, BF16_SUBLANES)
    nb_s = past // MOBA_BLOCK
    assert nbp_p + 4 <= LANES and nb_s + 4 <= LANES

    kb_p, cq_p, _ = _position_tables(S, nbp_p, T)
    kb_s, _, ab_s = _position_tables(past, nb_s, T)
    e2_s = kb_s.T
    group = jnp.arange(512) // (512 // CONV_GROUPS)
    gmat = jnp.where(group[:, None] == group[None, :], 1.0 / (512 // CONV_GROUPS), 0.0).astype(BF16)

    xp, xs = x_prompt, x_sample.reshape(DB * T, D)
    row = lambda a: a.reshape(1, -1)
    outs = {k: [] for k in ("kp", "vp", "cp", "fp", "ks", "vs", "cs", "fs")}
    for l in range(depth):
        w_in_b, w_out_b = w_in[l].astype(BF16), w_out[l].astype(BF16)
        w_up_b, w_down_b = w_up[l].astype(BF16), w_down[l].astype(BF16)
        wdw = jnp.pad(w_conv_dw[l], ((0, HALO - CONV_W), (0, 0)))
        wfd = jnp.pad(w_ffn_dw[l], ((0, 8 - FFN_CONV_W), (0, 0)))
        wdw_tiles = jnp.broadcast_to(wdw[:, None, :], (HALO, SUBLANES, 512))
        conv_args = (row(b_conv_dw[l]), row(g_conv_ln[l]), row(b_conv_ln[l]), row(g_conv_out[l]), gmat,
                     w_out_b, row(g_post_mix[l]))
        ffn_args = (row(g_pre_ffn[l]), w_up_b, wfd, row(b_ffn_dw[l]), w_down_b, row(g_post_ffn[l]))

        qT, selT, kbf, vT, kp, vp, u = _mix_prompt(xp, row(g_pre_mix[l]), w_in_b, nbp_p)
        attn = _attn_prompt(qT, selT, cq_p, kbf, vT, kb_p, g_attn_out[l].reshape(-1, 1))
        xp = _merge_prompt(xp, attn, u, wdw_tiles, *conv_args)
        xp, fst_p = _ffn_prompt(xp, *ffn_args)
        outs["kp"].append(kp.reshape(B, S, N_HEADS, HEAD_DIM))
        outs["vp"].append(vp.reshape(B, S, N_HEADS, HEAD_DIM))
        outs["cp"].append(u[:, S - (CONV_W - 1):])
        outs["fp"].append(fst_p)

        q_s, k_s, v_s, u_s = _mix_sample(xs, row(g_pre_mix[l]), w_in_b)
        as3 = lambda a: a.reshape(DB, T, ATTN_DIM)
        pages_t = lambda c: jnp.transpose(c, (0, 2, 3, 1)).reshape(n_pool, ATTN_DIM, page_size)
        attn_s = _attn_sample(page_table, as3(q_s), as3(k_s), as3(v_s), pages_t(cache_k[l]), pages_t(cache_v[l]),
                              ab_s, e2_s, row(g_attn_out[l]))
        ctx = jnp.concatenate([state_conv[l], u_s.reshape(DB, T, 512)], axis=1)
        xs = _merge_sample(xs, attn_s.reshape(DB * T, ATTN_DIM), ctx, wdw, *conv_args)
        xs, fst_s = _ffn_sample(xs, state_ffn_conv[l], *ffn_args)
        outs["ks"].append(k_s.reshape(DB, T, N_HEADS, HEAD_DIM))
        outs["vs"].append(v_s.reshape(DB, T, N_HEADS, HEAD_DIM))
        outs["cs"].append(ctx[:, T:])
        outs["fs"].append(fst_s)

    st = lambda name: jnp.stack(outs[name])
    return (xp, xs.reshape(DB, T, D), st("kp"), st("vp"), st("cp"), st("fp"),
            st("ks"), st("vs"), st("cs"), st("fs"))
```

```python
import functools

import jax
import jax.numpy as jnp
from jax import lax
from jax.experimental import pallas as pl
from jax.experimental.pallas import tpu as pltpu

N_HEADS = 8
HEAD_DIM = 64
ATTN_DIM = N_HEADS * HEAD_DIM
CONV_GROUPS = 8
CONV_W = 31
MOBA_BLOCK = 256
MOBA_TOPK = 3
FFN_CONV_W = 3
EPS = 1e-6
NEG = -1e30
LOG2E = 1.4426950408889634

LANES = 128
SUBLANES = 8
BF16_SUBLANES = 16
HALO = 32
PAGES_PER_STEP = 16
DMA_SLOTS = 4
PREFETCH_CHUNKS = DMA_SLOTS - 1
VMEM_LIMIT = 56 * 1024 * 1024

F32 = jnp.float32
BF16 = jnp.bfloat16
HIGHEST = lax.Precision.HIGHEST


def _round_up(n, m):
    return -(-n // m) * m


def _const_spec(shape):
    zeros = (0,) * len(shape)
    return pl.BlockSpec(shape, lambda *_: zeros)


def _rms(x, g):
    return x * lax.rsqrt(jnp.mean(x * x, axis=-1, keepdims=True) + EPS) * g


def _split_dot(x, w_bf16):
    hi = x.astype(BF16)
    lo = (x - hi.astype(F32)).astype(BF16)
    return (jnp.dot(hi, w_bf16, preferred_element_type=F32)
            + jnp.dot(lo, w_bf16, preferred_element_type=F32))


def _top_k_bias(sc, cand, axis):
    n = sc.shape[axis]
    idx = lax.broadcasted_iota(jnp.int32, sc.shape, axis)
    s = jnp.where(cand, sc, -jnp.inf)
    bias = jnp.full(sc.shape, NEG, F32)
    for _ in range(MOBA_TOPK):
        m = jnp.max(s, axis=axis, keepdims=True)
        first = jnp.min(jnp.where(s == m, idx, n), axis=axis, keepdims=True)
        pick = idx == first
        bias = jnp.where(pick, 0.0, bias)
        s = jnp.where(pick, -jnp.inf, s)
    return jnp.where(cand, bias, NEG)


def _mix_prompt_kernel(x_ref, g_ref, w_ref, qT_ref, selT_ref, kbf_ref, vT_ref, k_ref, v_ref, u_ref,
                       km_ref):
    j = pl.program_id(1)
    nbp = km_ref.shape[0]

    @pl.when(j == 0)
    def _():
        km_ref[...] = jnp.zeros_like(km_ref)

    h = _rms(x_ref[0], g_ref[...]).astype(BF16)
    z = jnp.dot(h, w_ref[...], preferred_element_type=F32)
    q = z[:, 0:ATTN_DIM] * (HEAD_DIM ** -0.5 * LOG2E)
    k = z[:, ATTN_DIM:2 * ATTN_DIM]
    v = z[:, 2 * ATTN_DIM:3 * ATTN_DIM]
    a = z[:, 3 * ATTN_DIM:3 * ATTN_DIM + 512]
    gg = z[:, 3 * ATTN_DIM + 512:]
    k_ref[0] = k
    v_ref[0] = v
    kbf_ref[0] = k.astype(BF16)
    u_ref[0] = a * jax.nn.sigmoid(gg)
    qT = q.T
    qT_ref[0] = qT.astype(BF16)
    vT_ref[0, 0] = v.T.astype(BF16)

    km = km_ref[...]
    blk = lax.broadcasted_iota(jnp.int32, (nbp, MOBA_BLOCK), 0)
    lane_head = lax.broadcasted_iota(jnp.int32, (nbp, LANES), 1) // HEAD_DIM
    for hh in range(N_HEADS):
        pr = hh // 2
        kmp = jnp.where(lane_head == hh % 2, km[:, pr * LANES:(pr + 1) * LANES], 0.0)
        sc = jnp.dot(kmp, qT[pr * LANES:(pr + 1) * LANES, :], preferred_element_type=F32,
                     precision=HIGHEST)
        selT_ref[0, hh] = _top_k_bias(sc, blk < j, 0).astype(BF16)

    row = lax.broadcasted_iota(jnp.int32, km.shape, 0)
    km_ref[...] = jnp.where(row == j, jnp.mean(k, axis=0, keepdims=True), km)


def _mix_prompt(x, g, w_bf16, nbp):
    B, S, D = x.shape
    nb = S // MOBA_BLOCK
    n_out = w_bf16.shape[1]
    tile = lambda b, j: (b, j, 0)
    return pl.pallas_call(
        _mix_prompt_kernel,
        grid=(B, nb),
        in_specs=[pl.BlockSpec((1, MOBA_BLOCK, D), tile), _const_spec((1, D)), _const_spec((D, n_out))],
        out_specs=[
            pl.BlockSpec((1, ATTN_DIM, MOBA_BLOCK), lambda b, j: (b, 0, j)),
            pl.BlockSpec((1, N_HEADS, nbp, MOBA_BLOCK), lambda b, j: (b, 0, 0, j)),
            pl.BlockSpec((1, MOBA_BLOCK, ATTN_DIM), tile),
            pl.BlockSpec((1, 1, ATTN_DIM, MOBA_BLOCK), lambda b, j: (b, j, 0, 0)),
            pl.BlockSpec((1, MOBA_BLOCK, ATTN_DIM), tile),
            pl.BlockSpec((1, MOBA_BLOCK, ATTN_DIM), tile),
            pl.BlockSpec((1, MOBA_BLOCK, 512), tile),
        ],
        out_shape=[
            jax.ShapeDtypeStruct((B, ATTN_DIM, S), BF16),
            jax.ShapeDtypeStruct((B, N_HEADS, nbp, S), BF16),
            jax.ShapeDtypeStruct((B, S, ATTN_DIM), BF16),
            jax.ShapeDtypeStruct((B, nb, ATTN_DIM, MOBA_BLOCK), BF16),
            jax.ShapeDtypeStruct((B, S, ATTN_DIM), F32),
            jax.ShapeDtypeStruct((B, S, ATTN_DIM), F32),
            jax.ShapeDtypeStruct((B, S, 512), F32),
        ],
        scratch_shapes=[pltpu.VMEM((nbp, ATTN_DIM), F32)],
        compiler_params=pltpu.CompilerParams(dimension_semantics=("arbitrary", "arbitrary"),
                                             vmem_limit_bytes=VMEM_LIMIT),
        name="mix_prompt",
    )(x, g, w_bf16)


def _attn_prompt_kernel(qT_ref, selT_ref, cq_ref, k_ref, vT_ref, kb_ref, g_ref, o_ref, s_buf, p_buf, *, unroll):
    nb = vT_ref.shape[1]
    nbp = selT_ref.shape[2]
    sub_head = lax.broadcasted_iota(jnp.int32, (LANES, MOBA_BLOCK), 0) // HEAD_DIM
    no_sel = jnp.zeros((nbp, MOBA_BLOCK), BF16)
    ones = jnp.ones((BF16_SUBLANES, MOBA_BLOCK), BF16)
    minus_inf = [jnp.full((1, MOBA_BLOCK), -jnp.inf, F32)] * 2

    def operands(j):
        off = pl.multiple_of(j * MOBA_BLOCK, MOBA_BLOCK)
        qT = qT_ref[0, :, pl.ds(off, MOBA_BLOCK)]
        qq, qq_own = [], []
        for hh in range(2):
            qm = jnp.where(sub_head == hh, qT, jnp.zeros_like(qT))
            qq.append(jnp.concatenate([qm, selT_ref[0, hh, :, pl.ds(off, MOBA_BLOCK)], cq_ref[hh]], axis=0))
            qq_own.append(jnp.concatenate([qm, no_sel, cq_ref[hh]], axis=0))
        return qq, qq_own

    def stage_a(b, slot, rhs, causal):
        off = pl.multiple_of(b * MOBA_BLOCK, MOBA_BLOCK)
        kk = jnp.concatenate([k_ref[0, pl.ds(off, MOBA_BLOCK), :], kb_ref[pl.ds(off, MOBA_BLOCK), :]], axis=1)
        col_max = []
        for hh in range(2):
            s = jnp.dot(kk, rhs[hh], preferred_element_type=F32)
            if causal:
                key_i = lax.broadcasted_iota(jnp.int32, s.shape, 0)
                qry_i = lax.broadcasted_iota(jnp.int32, s.shape, 1)
                s = jnp.where(key_i <= qry_i, s, NEG)
            s_buf[slot, hh] = s
            col_max.append(jnp.max(s, axis=0, keepdims=True))
        return col_max

    def stage_b(slot, col_max, m):
        m_out, alpha_out = [], []
        for hh in range(2):
            m_new = jnp.maximum(m[hh], col_max[hh])
            p_buf[slot, hh] = jnp.exp2(s_buf[slot, hh] - m_new).astype(BF16)
            m_out.append(m_new)
            alpha_out.append(jnp.exp2(m[hh] - m_new))
        return m_out, alpha_out

    def stage_c(b, slot, alpha, acc):
        vt = vT_ref[0, b]
        return [alpha[hh] * acc[hh]
                + jnp.dot(jnp.concatenate([vt[hh * HEAD_DIM:(hh + 1) * HEAD_DIM, :], ones], axis=0),
                          p_buf[slot, hh], preferred_element_type=F32) for hh in range(2)]

    def finish(j, acc):
        outs = []
        for hh in range(2):
            o = acc[hh][0:HEAD_DIM] / acc[hh][HEAD_DIM:HEAD_DIM + 1]
            outs.append(o * lax.rsqrt(jnp.mean(o * o, axis=0, keepdims=True) + EPS))
        on = jnp.concatenate(outs, axis=0) * g_ref[...]
        o_ref[0, pl.ds(pl.multiple_of(j * MOBA_BLOCK, MOBA_BLOCK), MOBA_BLOCK), :] = on.T.astype(o_ref.dtype)

    def query_block(j, pending):
        acc_p, alpha_p, last_p = pending
        finish(jnp.maximum(j - 1, 0), stage_c(last_p, 1, alpha_p, acc_p))
        qq, qq_own = operands(j)
        m, alpha_c = stage_b(1, stage_a(j, 1, qq_own, True), minus_inf)
        cm0 = stage_a(0, 0, qq, False)
        acc = [jnp.zeros((HEAD_DIM + BF16_SUBLANES, MOBA_BLOCK), F32)] * 2

        def trip(t, carry):
            m, acc, col_max, alpha = carry
            for u in range(unroll):
                b, slot = unroll * t + u, u % 2
                nxt = b + 1 if u < unroll - 1 else jnp.minimum(b + 1, nb - 1)
                prev = b - 1 if u > 0 else jnp.where(t == 0, j, b - 1)
                next_max = stage_a(nxt, 1 - slot, qq, False)
                m, alpha_b = stage_b(slot, col_max, m)
                acc = stage_c(prev, 1 - slot, alpha, acc)
                col_max, alpha = next_max, alpha_b
            return m, acc, col_max, alpha

        n_trips = (j + unroll - 1) // unroll
        m, acc, cm0, alpha_c = lax.fori_loop(0, n_trips, trip, (m, acc, cm0, alpha_c))
        return acc, alpha_c, jnp.where(n_trips == 0, j, unroll * n_trips - 1)

    p_buf[1] = jnp.zeros(p_buf.shape[1:], BF16)
    start = ([jnp.ones((HEAD_DIM + BF16_SUBLANES, MOBA_BLOCK), F32)] * 2, [jnp.ones((1, MOBA_BLOCK), F32)] * 2,
             jnp.int32(0))
    acc, alpha_c, last = lax.fori_loop(0, nb, query_block, start)
    finish(nb - 1, stage_c(last, 1, alpha_c, acc))


def _attn_prompt(qT, selT, cq, kbf, vT, kb, g_col):
    B, _, S = qT.shape
    nb = S // MOBA_BLOCK
    nbp = selT.shape[2]
    unroll = 4 if nb % 4 == 0 else 2
    return pl.pallas_call(
        functools.partial(_attn_prompt_kernel, unroll=unroll),
        grid=(B, N_HEADS // 2),
        in_specs=[
            pl.BlockSpec((1, LANES, S), lambda b, p: (b, p, 0)),
            pl.BlockSpec((1, 2, nbp, S), lambda b, p: (b, p, 0, 0)),
            pl.BlockSpec((2, LANES - nbp, MOBA_BLOCK), lambda b, p: (p, 0, 0)),
            pl.BlockSpec((1, S, LANES), lambda b, p: (b, 0, p)),
            pl.BlockSpec((1, nb, LANES, MOBA_BLOCK), lambda b, p: (b, 0, p, 0)),
            _const_spec((S, LANES)),
            pl.BlockSpec((LANES, 1), lambda b, p: (p, 0)),
        ],
        out_specs=pl.BlockSpec((1, S, LANES), lambda b, p: (b, 0, p)),
        out_shape=jax.ShapeDtypeStruct((B, S, ATTN_DIM), BF16),
        scratch_shapes=[pltpu.VMEM((2, 2, MOBA_BLOCK, MOBA_BLOCK), F32),
                        pltpu.VMEM((2, 2, MOBA_BLOCK, MOBA_BLOCK), BF16)],
        compiler_params=pltpu.CompilerParams(dimension_semantics=("arbitrary",) * 2,
                                             vmem_limit_bytes=VMEM_LIMIT),
        name="attn_prompt",
    )(qT, selT, cq, kbf, vT, kb, g_col)


def _conv_branch_out(c, lng, lnb, gco, gmat):
    mu = jnp.mean(c, axis=-1, keepdims=True)
    d = c - mu
    y = d * lax.rsqrt(jnp.mean(d * d, axis=-1, keepdims=True) + EPS) * lng + lnb
    y = y * jax.nn.sigmoid(y)
    gms = _split_dot(y * y, gmat)
    return (y * lax.rsqrt(gms + EPS) * gco).astype(BF16)


def _merge_tail(x, attn_n, conv_n, wout_ref, gpost):
    o = (jnp.dot(attn_n, wout_ref[0:ATTN_DIM, :], preferred_element_type=F32)
         + jnp.dot(conv_n, wout_ref[ATTN_DIM:, :], preferred_element_type=F32))
    return x + _rms(o, gpost)


def _merge_prompt_kernel(x_ref, attn_ref, u_ref, uh_ref, wdw_ref, bdw_ref, lng_ref, lnb_ref, gco_ref,
                         gmat_ref, wout_ref, gpost_ref, o_ref, ubuf, sbuf, cbuf, *, rows, chunk):
    i = pl.program_id(1)
    ubuf[0:HALO, :] = jnp.where(i > 0, uh_ref[0], 0.0)
    ubuf[HALO:, :] = u_ref[0]
    span = sbuf.shape[1]
    for shift in range(1, SUBLANES):
        sbuf[shift - 1] = ubuf[shift:shift + span, :]
    first = HALO - (CONV_W - 1)
    tiles = chunk // SUBLANES
    for r0 in range(0, rows, chunk):
        acc = jnp.broadcast_to(bdw_ref[...], (tiles, SUBLANES, 512))
        for w in range(CONV_W):
            shift, base = (first + w) % SUBLANES, (first + w) // SUBLANES * SUBLANES + r0
            window = ubuf[base:base + chunk, :] if shift == 0 else sbuf[shift - 1, base:base + chunk, :]
            acc = acc + wdw_ref[w] * window.reshape(tiles, SUBLANES, 512)
        cbuf[r0:r0 + chunk, :] = acc.reshape(chunk, 512)
    conv_n = _conv_branch_out(cbuf[...], lng_ref[...], lnb_ref[...], gco_ref[...], gmat_ref[...])
    o_ref[0] = _merge_tail(x_ref[0], attn_ref[0], conv_n, wout_ref, gpost_ref[...])


def _merge_prompt(x, attn_n, u, wdw, bdw, lng, lnb, gco, gmat, wout_bf16, gpost, rows=256, chunk=32):
    B, S, D = x.shape
    tile = lambda b, i: (b, i, 0)
    halo_blocks = rows // HALO
    return pl.pallas_call(
        functools.partial(_merge_prompt_kernel, rows=rows, chunk=chunk),
        grid=(B, S // rows),
        in_specs=[
            pl.BlockSpec((1, rows, D), tile),
            pl.BlockSpec((1, rows, ATTN_DIM), tile),
            pl.BlockSpec((1, rows, 512), tile),
            pl.BlockSpec((1, HALO, 512), lambda b, i: (b, jnp.maximum(i * halo_blocks - 1, 0), 0)),
            _const_spec(wdw.shape), _const_spec((1, 512)), _const_spec((1, 512)), _const_spec((1, 512)),
            _const_spec((1, 512)), _const_spec((512, 512)), _const_spec(wout_bf16.shape), _const_spec((1, D)),
        ],
        out_specs=pl.BlockSpec((1, rows, D), tile),
        out_shape=jax.ShapeDtypeStruct((B, S, D), F32),
        scratch_shapes=[pltpu.VMEM((HALO + rows, 512), F32),
                        pltpu.VMEM((SUBLANES - 1, HALO + rows - SUBLANES, 512), F32),
                        pltpu.VMEM((rows, 512), F32)],
        compiler_params=pltpu.CompilerParams(dimension_semantics=("arbitrary", "arbitrary"),
                                             vmem_limit_bytes=VMEM_LIMIT),
        name="merge_prompt",
    )(x, attn_n, u, u, wdw, bdw, lng, lnb, gco, gmat, wout_bf16, gpost)


def _merge_sample_kernel(x_ref, attn_ref, ctx_ref, wdw_ref, bdw_ref, lng_ref, lnb_ref, gco_ref,
                         gmat_ref, wout_ref, gpost_ref, o_ref, cbuf, *, n_seq, t_new):
    def one_seq(s, _):
        acc = jnp.broadcast_to(bdw_ref[...], (t_new, 512))
        for w in range(CONV_W):
            acc = acc + wdw_ref[w:w + 1, :] * ctx_ref[s, w:w + t_new, :]
        cbuf[pl.ds(pl.multiple_of(s * t_new, t_new), t_new), :] = acc
        return 0

    lax.fori_loop(0, n_seq, one_seq, 0)
    conv_n = _conv_branch_out(cbuf[...], lng_ref[...], lnb_ref[...], gco_ref[...], gmat_ref[...])
    o_ref[...] = _merge_tail(x_ref[...], attn_ref[...], conv_n, wout_ref, gpost_ref[...])


def _merge_sample(x2d, attn_n, ctx, wdw, bdw, lng, lnb, gco, gmat, wout_bf16, gpost, n_seq=32):
    DB, ctx_len, _ = ctx.shape
    n_seq = min(n_seq, DB)
    t_new = ctx_len - (CONV_W - 1)
    D = x2d.shape[1]
    rows = n_seq * t_new
    tile = lambda i: (i, 0)
    return pl.pallas_call(
        functools.partial(_merge_sample_kernel, n_seq=n_seq, t_new=t_new),
        grid=(DB // n_seq,),
        in_specs=[
            pl.BlockSpec((rows, D), tile),
            pl.BlockSpec((rows, ATTN_DIM), tile),
            pl.BlockSpec((n_seq, ctx_len, 512), lambda i: (i, 0, 0)),
            _const_spec(wdw.shape), _const_spec((1, 512)), _const_spec((1, 512)), _const_spec((1, 512)),
            _const_spec((1, 512)), _const_spec((512, 512)), _const_spec(wout_bf16.shape), _const_spec((1, D)),
        ],
        out_specs=pl.BlockSpec((rows, D), tile),
        out_shape=jax.ShapeDtypeStruct(x2d.shape, F32),
        scratch_shapes=[pltpu.VMEM((rows, 512), F32)],
        compiler_params=pltpu.CompilerParams(dimension_semantics=("arbitrary",),
                                             vmem_limit_bytes=VMEM_LIMIT),
        name="merge_sample",
    )(x2d, attn_n, ctx, wdw, bdw, lng, lnb, gco, gmat, wout_bf16, gpost)


def _gated(c_gate, c_val):
    return (jax.nn.gelu(c_gate, approximate=True) * c_val).astype(BF16)


def _ffn_prompt_kernel(x_ref, gpre_ref, wup_ref, wdw_ref, bdw_ref, wdown_ref, gpost_ref, o_ref, st_ref,
                       upbuf, fbuf, *, rows, d_ff, chunk):
    i = pl.program_id(1)

    @pl.when(i == 0)
    def _():
        upbuf[0:8, :] = jnp.zeros((8, 2 * d_ff), F32)

    @pl.when(i > 0)
    def _():
        upbuf[0:8, :] = upbuf[rows:rows + 8, :]

    x = x_ref[0]
    h = _rms(x, gpre_ref[...]).astype(BF16)
    upbuf[8:, :] = jnp.dot(h, wup_ref[...], preferred_element_type=F32)
    st_ref[0] = upbuf[rows + 8 - (FFN_CONV_W - 1):rows + 8, :]
    first = 8 - (FFN_CONV_W - 1)
    for c0 in range(0, d_ff, chunk):
        halves = []
        for base in (c0, d_ff + c0):
            acc = jnp.broadcast_to(bdw_ref[:, base:base + chunk], (rows, chunk))
            for w in range(FFN_CONV_W):
                acc = acc + wdw_ref[w:w + 1, base:base + chunk] * upbuf[first + w:first + w + rows, base:base + chunk]
            halves.append(acc)
        fbuf[:, c0:c0 + chunk] = _gated(*halves)
    f = jnp.dot(fbuf[...], wdown_ref[...], preferred_element_type=F32)
    o_ref[0] = x + _rms(f, gpost_ref[...])


def _ffn_prompt(x, gpre, wup_bf16, wdw, bdw, wdown_bf16, gpost, rows=256, chunk=256):
    B, S, D = x.shape
    d_ff = wdown_bf16.shape[0]
    tile = lambda b, i: (b, i, 0)
    return pl.pallas_call(
        functools.partial(_ffn_prompt_kernel, rows=rows, d_ff=d_ff, chunk=chunk),
        grid=(B, S // rows),
        in_specs=[
            pl.BlockSpec((1, rows, D), tile), _const_spec((1, D)), _const_spec(wup_bf16.shape),
            _const_spec(wdw.shape), _const_spec((1, 2 * d_ff)), _const_spec(wdown_bf16.shape),
            _const_spec((1, D)),
        ],
        out_specs=[pl.BlockSpec((1, rows, D), tile),
                   pl.BlockSpec((1, FFN_CONV_W - 1, 2 * d_ff), lambda b, i: (b, 0, 0))],
        out_shape=[jax.ShapeDtypeStruct((B, S, D), F32),
                   jax.ShapeDtypeStruct((B, FFN_CONV_W - 1, 2 * d_ff), F32)],
        scratch_shapes=[pltpu.VMEM((8 + rows, 2 * d_ff), F32), pltpu.VMEM((rows, d_ff), BF16)],
        compiler_params=pltpu.CompilerParams(dimension_semantics=("arbitrary", "arbitrary"),
                                             vmem_limit_bytes=VMEM_LIMIT),
        name="ffn_prompt",
    )(x, gpre, wup_bf16, wdw, bdw, wdown_bf16, gpost)


def _ffn_sample_kernel(x_ref, st_in_ref, gpre_ref, wup_ref, wdw_ref, bdw_ref, wdown_ref, gpost_ref,
                       o_ref, st_ref, upbuf, fbuf, *, n_seq, t_new, d_ff, chunk):
    x = x_ref[...]
    h = _rms(x, gpre_ref[...]).astype(BF16)
    up = jnp.dot(h, wup_ref[...], preferred_element_type=F32)
    first = 8 - (FFN_CONV_W - 1)
    upbuf[:, first:8, :] = st_in_ref[...]
    upbuf[:, 8:, :] = up.reshape(n_seq, t_new, 2 * d_ff)
    st_ref[...] = upbuf[:, 8 + t_new - (FFN_CONV_W - 1):8 + t_new, :]
    for c0 in range(0, d_ff, chunk):
        halves = []
        for base in (c0, d_ff + c0):
            acc = jnp.broadcast_to(bdw_ref[:, base:base + chunk].reshape(1, 1, chunk), (n_seq, t_new, chunk))
            for w in range(FFN_CONV_W):
                acc = acc + (wdw_ref[w:w + 1, base:base + chunk].reshape(1, 1, chunk)
                             * upbuf[:, first + w:first + w + t_new, base:base + chunk])
            halves.append(acc.reshape(n_seq * t_new, chunk))
        fbuf[:, c0:c0 + chunk] = _gated(*halves)
    f = jnp.dot(fbuf[...], wdown_ref[...], preferred_element_type=F32)
    o_ref[...] = x + _rms(f, gpost_ref[...])


def _ffn_sample(x2d, st_in, gpre, wup_bf16, wdw, bdw, wdown_bf16, gpost, n_seq=16, chunk=256):
    DB = st_in.shape[0]
    n_seq = min(n_seq, DB)
    D = x2d.shape[1]
    t_new = x2d.shape[0] // DB
    d_ff = wdown_bf16.shape[0]
    rows = n_seq * t_new
    return pl.pallas_call(
        functools.partial(_ffn_sample_kernel, n_seq=n_seq, t_new=t_new, d_ff=d_ff, chunk=chunk),
        grid=(DB // n_seq,),
        in_specs=[
            pl.BlockSpec((rows, D), lambda i: (i, 0)),
            pl.BlockSpec((n_seq, FFN_CONV_W - 1, 2 * d_ff), lambda i: (i, 0, 0)),
            _const_spec((1, D)), _const_spec(wup_bf16.shape), _const_spec(wdw.shape),
            _const_spec((1, 2 * d_ff)), _const_spec(wdown_bf16.shape), _const_spec((1, D)),
        ],
        out_specs=[pl.BlockSpec((rows, D), lambda i: (i, 0)),
                   pl.BlockSpec((n_seq, FFN_CONV_W - 1, 2 * d_ff), lambda i: (i, 0, 0))],
        out_shape=[jax.ShapeDtypeStruct(x2d.shape, F32),
                   jax.ShapeDtypeStruct((DB, FFN_CONV_W - 1, 2 * d_ff), F32)],
        scratch_shapes=[pltpu.VMEM((n_seq, 8 + t_new, 2 * d_ff), F32), pltpu.VMEM((rows, d_ff), BF16)],
        compiler_params=pltpu.CompilerParams(dimension_semantics=("arbitrary",),
                                             vmem_limit_bytes=VMEM_LIMIT),
        name="ffn_sample",
    )(x2d, st_in, gpre, wup_bf16, wdw, bdw, wdown_bf16, gpost)


def _mix_sample_kernel(x_ref, g_ref, w_ref, q_ref, k_ref, v_ref, u_ref):
    h = _rms(x_ref[...], g_ref[...]).astype(BF16)
    z = jnp.dot(h, w_ref[...], preferred_element_type=F32)
    q_ref[...] = z[:, 0:ATTN_DIM] * (HEAD_DIM ** -0.5)
    k_ref[...] = z[:, ATTN_DIM:2 * ATTN_DIM]
    v_ref[...] = z[:, 2 * ATTN_DIM:3 * ATTN_DIM]
    u_ref[...] = z[:, 3 * ATTN_DIM:3 * ATTN_DIM + 512] * jax.nn.sigmoid(z[:, 3 * ATTN_DIM + 512:])


def _mix_sample(x2d, g, w_bf16, rows=256):
    N, D = x2d.shape
    rows = min(rows, N)
    tile = lambda i: (i, 0)
    out = jax.ShapeDtypeStruct((N, ATTN_DIM), F32)
    return pl.pallas_call(
        _mix_sample_kernel,
        grid=(N // rows,),
        in_specs=[pl.BlockSpec((rows, D), tile), _const_spec((1, D)), _const_spec(w_bf16.shape)],
        out_specs=[pl.BlockSpec((rows, ATTN_DIM), tile)] * 4,
        out_shape=[out] * 4,
        compiler_params=pltpu.CompilerParams(dimension_semantics=("arbitrary",),
                                             vmem_limit_bytes=VMEM_LIMIT),
        name="mix_sample",
    )(x2d, g, w_bf16)


def _attn_sample_kernel(pt_ref, q_ref, kn_ref, vn_ref, ab_ref, e2_ref, g_ref, kT_hbm, vT_hbm, o_ref,
                        page_buf, sems, logit_buf, p_buf, *, n_pages, past, t_new):
    b = pl.program_id(0)
    n_seq = pl.num_programs(0)
    pages = PAGES_PER_STEP
    n_kv = n_pages // pages
    n_chunks = 2 * n_kv
    page_size = page_buf.shape[3]
    per_block = MOBA_BLOCK // page_size
    blocks_per_chunk = pages // per_block
    nb = n_pages // per_block
    n_col = N_HEADS * t_new

    def page_copy(seq, chunk, p):
        src = kT_hbm if chunk < n_kv else vT_hbm
        page = pt_ref[seq * n_pages + (chunk % n_kv) * pages + p]
        slot = chunk % DMA_SLOTS
        return pltpu.make_async_copy(src.at[page], page_buf.at[slot, p], sems.at[slot])

    def start_chunk(seq, chunk):
        for p in range(pages):
            page_copy(seq, chunk, p).start()

    def wait_chunk(seq, chunk):
        for p in range(pages):
            page_copy(seq, chunk, p).wait()

    def prefetch(chunk):
        if chunk < n_chunks:
            start_chunk(b, chunk)
        else:
            @pl.when(b + 1 < n_seq)
            def _():
                start_chunk(b + 1, chunk - n_chunks)

    @pl.when(b == 0)
    def _():
        for c in range(PREFETCH_CHUNKS):
            start_chunk(0, c)

    q = q_ref[0]
    row_head = lax.broadcasted_iota(jnp.int32, (n_col, ATTN_DIM), 0) // t_new
    lane_head = lax.broadcasted_iota(jnp.int32, (n_col, ATTN_DIM), 1) // HEAD_DIM
    own = row_head == lane_head
    qbd = jnp.where(own, jnp.concatenate([q] * N_HEADS, axis=0), 0.0)
    qb = qbd.astype(BF16)

    sc = jnp.zeros((n_col, LANES), F32)
    lane = lax.broadcasted_iota(jnp.int32, sc.shape, 1)
    for c in range(n_kv):
        prefetch(c + PREFETCH_CHUNKS)
        wait_chunk(b, c)
        for blk in range(blocks_per_chunk):
            block_sum = None
            for p in range(blk * per_block, (blk + 1) * per_block):
                lg = jnp.dot(qb, page_buf[c % DMA_SLOTS, p].astype(BF16), preferred_element_type=F32)
                logit_buf[c * pages + p] = lg
                block_sum = lg if block_sum is None else block_sum + lg
            score = jnp.sum(block_sum, axis=1, keepdims=True) * (1.0 / MOBA_BLOCK)
            sc = jnp.where(lane == c * blocks_per_chunk + blk, score, sc)

    ab = ab_ref[...]
    coef = jnp.where(lane < nb, _top_k_bias(sc, lane < nb, 1), ab).astype(BF16)
    bias = jnp.dot(coef, e2_ref[...], preferred_element_type=F32)
    l_own = lax.dot_general(qb, kn_ref[0].astype(BF16), (((1,), (1,)), ((), ())), preferred_element_type=F32)
    key_t = lax.broadcasted_iota(jnp.int32, (n_col, t_new), 1)
    qry_t = lax.broadcasted_iota(jnp.int32, (n_col, t_new), 0) % t_new
    l_own = l_own + ab[:, nb:nb + 1] * (past + key_t).astype(F32)
    l_own = jnp.where(key_t <= qry_t, l_own, NEG)
    lane_max = None
    for pg in range(n_pages):
        lg = logit_buf[pg] + bias[:, pg * page_size:(pg + 1) * page_size]
        logit_buf[pg] = lg
        lane_max = lg if lane_max is None else jnp.maximum(lane_max, lg)
    m = jnp.maximum(jnp.max(l_own, axis=1, keepdims=True), jnp.max(lane_max, axis=1, keepdims=True))
    p_own = jnp.exp(l_own - m)
    lane_sum = jnp.zeros((n_col, page_size), F32)
    for pg in range(n_pages):
        pe = jnp.exp(logit_buf[pg] - m)
        lane_sum = lane_sum + pe
        p_buf[pg] = pe.astype(BF16)
    denom = jnp.sum(p_own, axis=1, keepdims=True) + jnp.sum(lane_sum, axis=1, keepdims=True)

    acc = jnp.dot(p_own.astype(BF16), vn_ref[0].astype(BF16), preferred_element_type=F32)
    for c in range(n_kv, n_chunks):
        prefetch(c + PREFETCH_CHUNKS)
        wait_chunk(b, c)
        for p in range(pages):
            acc = acc + lax.dot_general(p_buf[(c - n_kv) * pages + p], page_buf[c % DMA_SLOTS, p].astype(BF16),
                                        (((1,), (1,)), ((), ())), preferred_element_type=F32)

    o = jnp.where(own, acc / denom, 0.0)
    on = o * lax.rsqrt(jnp.sum(o * o, axis=1, keepdims=True) * (1.0 / HEAD_DIM) + EPS)
    out = on[0:t_new]
    for hh in range(1, N_HEADS):
        out = out + on[hh * t_new:(hh + 1) * t_new]
    o_ref[0] = (out * g_ref[...]).astype(o_ref.dtype)


def _attn_sample(page_table, q, kn, vn, cache_kT, cache_vT, ab, e2, g_attn):
    DB, t_new, _ = q.shape
    n_pages = page_table.shape[1]
    page_size = cache_kT.shape[2]
    n_col = N_HEADS * t_new
    per_seq = lambda b, pt: (b, 0, 0)
    grid_spec = pltpu.PrefetchScalarGridSpec(
        num_scalar_prefetch=1,
        grid=(DB,),
        in_specs=[
            pl.BlockSpec((1, t_new, ATTN_DIM), per_seq),
            pl.BlockSpec((1, t_new, ATTN_DIM), per_seq),
            pl.BlockSpec((1, t_new, ATTN_DIM), per_seq),
            _const_spec(ab.shape), _const_spec(e2.shape), _const_spec((1, ATTN_DIM)),
            pl.BlockSpec(memory_space=pl.ANY), pl.BlockSpec(memory_space=pl.ANY),
        ],
        out_specs=pl.BlockSpec((1, t_new, ATTN_DIM), per_seq),
        scratch_shapes=[
            pltpu.VMEM((DMA_SLOTS, PAGES_PER_STEP, ATTN_DIM, page_size), F32),
            pltpu.SemaphoreType.DMA((DMA_SLOTS,)),
            pltpu.VMEM((n_pages, n_col, page_size), F32),
            pltpu.VMEM((n_pages, n_col, page_size), BF16),
        ],
    )
    return pl.pallas_call(
        functools.partial(_attn_sample_kernel, n_pages=n_pages, past=n_pages * page_size, t_new=t_new),
        grid_spec=grid_spec,
        out_shape=jax.ShapeDtypeStruct((DB, t_new, ATTN_DIM), BF16),
        compiler_params=pltpu.CompilerParams(dimension_semantics=("arbitrary",),
                                             vmem_limit_bytes=VMEM_LIMIT),
        name="attn_sample",
    )(page_table.reshape(-1), q, kn, vn, ab, e2, g_attn, cache_kT, cache_vT)


def _alibi_slopes():
    return jnp.exp2(-8.0 * jnp.arange(1, N_HEADS + 1, dtype=F32) / N_HEADS)


def _position_tables(n_keys, nbp, t_new):
    slopes = _alibi_slopes()
    pos = jnp.arange(n_keys)
    blk, off = pos // MOBA_BLOCK, pos % MOBA_BLOCK
    lane = jnp.arange(LANES)
    kb = jnp.where(lane[None, :] == blk[:, None], 1.0, 0.0)
    for first in (nbp, nbp + 2):
        kb = jnp.where(lane[None, :] == first, off[:, None].astype(F32), kb)
        kb = jnp.where(lane[None, :] == first + 1, blk[:, None].astype(F32), kb)
    hi = (slopes * LOG2E).astype(BF16).astype(F32)
    lo = slopes * LOG2E - hi
    row = jnp.arange(LANES - nbp)[None, :]
    cq = (jnp.where(row == 0, hi[:, None], 0.0) + jnp.where(row == 1, hi[:, None] * MOBA_BLOCK, 0.0)
          + jnp.where(row == 2, lo[:, None], 0.0) + jnp.where(row == 3, lo[:, None] * MOBA_BLOCK, 0.0))
    cq = jnp.broadcast_to(cq[:, :, None], (N_HEADS, LANES - nbp, MOBA_BLOCK))
    ab = jnp.zeros((N_HEADS * t_new, LANES), F32)
    ab = ab.at[:, nbp].set(jnp.repeat(slopes, t_new)).at[:, nbp + 1].set(jnp.repeat(slopes, t_new) * MOBA_BLOCK)
    return kb.astype(BF16), cq.astype(BF16), ab


def kernel(x_prompt, x_sample, cache_k, cache_v, page_table, state_conv, state_ffn_conv, g_pre_mix, w_in,
           w_conv_dw, b_conv_dw, g_conv_ln, b_conv_ln, g_attn_out, g_conv_out, w_out, g_post_mix, g_pre_ffn,
           w_up, w_ffn_dw, b_ffn_dw, w_down, g_post_ffn):
    depth = w_in.shape[0]
    B, S, D = x_prompt.shape
    DB, T, _ = x_sample.shape
    n_pool, page_size = cache_k.shape[1], cache_k.shape[2]
    n_pages = page_table.shape[1]
    past = n_pages * page_size
    assert S % MOBA_BLOCK == 0 and S >= CONV_W - 1
    assert past % MOBA_BLOCK == 0 and past // MOBA_BLOCK >= MOBA_TOPK, "the cached past must be whole key blocks"
    assert n_pages % PAGES_PER_STEP == 0 and (PAGES_PER_STEP * page_size) % MOBA_BLOCK == 0
    assert (2 * n_pages // PAGES_PER_STEP) % DMA_SLOTS == 0, "ring slots must line up across sequences"
    assert T <= MOBA_BLOCK
    nb_p = S // MOBA_BLOCK
    nbp_p = _round_up(nb_p, BF16_SUBLANES)
    nb_s = past // MOBA_BLOCK
    assert nbp_p + 4 <= LANES and nb_s + 4 <= LANES

    kb_p, cq_p, _ = _position_tables(S, nbp_p, T)
    kb_s, _, ab_s = _position_tables(past, nb_s, T)
    e2_s = kb_s.T
    group = jnp.arange(512) // (512 // CONV_GROUPS)
    gmat = jnp.where(group[:, None] == group[None, :], 1.0 / (512 // CONV_GROUPS), 0.0).astype(BF16)

    xp, xs = x_prompt, x_sample.reshape(DB * T, D)
    row = lambda a: a.reshape(1, -1)
    outs = {k: [] for k in ("kp", "vp", "cp", "fp", "ks", "vs", "cs", "fs")}
    for l in range(depth):
        w_in_b, w_out_b = w_in[l].astype(BF16), w_out[l].astype(BF16)
        w_up_b, w_down_b = w_up[l].astype(BF16), w_down[l].astype(BF16)
        wdw = jnp.pad(w_conv_dw[l], ((0, HALO - CONV_W), (0, 0)))
        wfd = jnp.pad(w_ffn_dw[l], ((0, 8 - FFN_CONV_W), (0, 0)))
        wdw_tiles = jnp.broadcast_to(wdw[:, None, :], (HALO, SUBLANES, 512))
        conv_args = (row(b_conv_dw[l]), row(g_conv_ln[l]), row(b_conv_ln[l]), row(g_conv_out[l]), gmat,
                     w_out_b, row(g_post_mix[l]))
        ffn_args = (row(g_pre_ffn[l]), w_up_b, wfd, row(b_ffn_dw[l]), w_down_b, row(g_post_ffn[l]))

        qT, selT, kbf, vT, kp, vp, u = _mix_prompt(xp, row(g_pre_mix[l]), w_in_b, nbp_p)
        attn = _attn_prompt(qT, selT, cq_p, kbf, vT, kb_p, g_attn_out[l].reshape(-1, 1))
        xp = _merge_prompt(xp, attn, u, wdw_tiles, *conv_args)
        xp, fst_p = _ffn_prompt(xp, *ffn_args)
        outs["kp"].append(kp.reshape(B, S, N_HEADS, HEAD_DIM))
        outs["vp"].append(vp.reshape(B, S, N_HEADS, HEAD_DIM))
        outs["cp"].append(u[:, S - (CONV_W - 1):])
        outs["fp"].append(fst_p)

        q_s, k_s, v_s, u_s = _mix_sample(xs, row(g_pre_mix[l]), w_in_b)
        as3 = lambda a: a.reshape(DB, T, ATTN_DIM)
        pages_t = lambda c: jnp.transpose(c, (0, 2, 3, 1)).reshape(n_pool, ATTN_DIM, page_size)
        attn_s = _attn_sample(page_table, as3(q_s), as3(k_s), as3(v_s), pages_t(cache_k[l]), pages_t(cache_v[l]),
                              ab_s, e2_s, row(g_attn_out[l]))
        ctx = jnp.concatenate([state_conv[l], u_s.reshape(DB, T, 512)], axis=1)
        xs = _merge_sample(xs, attn_s.reshape(DB * T, ATTN_DIM), ctx, wdw, *conv_args)
        xs, fst_s = _ffn_sample(xs, state_ffn_conv[l], *ffn_args)
        outs["ks"].append(k_s.reshape(DB, T, N_HEADS, HEAD_DIM))
        outs["vs"].append(v_s.reshape(DB, T, N_HEADS, HEAD_DIM))
        outs["cs"].append(ctx[:, T:])
        outs["fs"].append(fst_s)

    st = lambda name: jnp.stack(outs[name])
    return (xp, xs.reshape(DB, T, D), st("kp"), st("vp"), st("cp"), st("fp"),
            st("ks"), st("vs"), st("cs"), st("fs"))
```
